```python
import jax, jax.numpy as jnp
from jax import lax
import numpy as np

D_MODEL = 2048
BATCH = 4
SEQ = 4096
DEPTH = 4

GRID_W = 64
CTX_LEN = 256
N_HEADS = 16
HEAD_DIM = D_MODEL // N_HEADS
ATTN_W = N_HEADS * HEAD_DIM
CONV_W = D_MODEL
CONV_K = 31
WIN_H = 8
WIN_W = 16
Q_BLOCK_W = 16
KEY_BLOCK_W = 2 * WIN_W
N_MOD = 6
D_FF = 5632
N_EXPERTS = 8
TOP_K = 2
D_FF_EXPERT = 4096
N_DENSE = (DEPTH + 1) // 2
N_MOE = DEPTH // 2
EPS = 1e-6
IN_COLS = 3 * ATTN_W + 2 * CONV_W + 2 * D_MODEL
IN_SPLITS = [ATTN_W, 2 * ATTN_W, 3 * ATTN_W, 3 * ATTN_W + 2 * CONV_W, 3 * ATTN_W + 2 * CONV_W + D_MODEL]

kernel_name = "hybrid_natten_conformer_moe_dit"


def rms_norm(x, g):
    xf = x.astype(jnp.float32)
    y = xf * lax.rsqrt(jnp.mean(xf * xf, axis=-1, keepdims=True) + EPS)
    return (y * g.astype(jnp.float32)).astype(x.dtype)


def layer_norm(x, g, b):
    xf = x.astype(jnp.float32)
    mu = jnp.mean(xf, axis=-1, keepdims=True)
    xc = xf - mu
    var = jnp.mean(xc * xc, axis=-1, keepdims=True)
    return (xc * lax.rsqrt(var + EPS) * g.astype(jnp.float32) + b.astype(jnp.float32)).astype(x.dtype)


def modulate(h, shift, scale):
    return h * (1.0 + scale) + shift


def split_heads(t):
    return t.reshape(t.shape[:-1] + (N_HEADS, HEAD_DIM))


def column_tables():
    nqb = GRID_W // Q_BLOCK_W
    qcol = np.arange(nqb)[:, None] * Q_BLOCK_W + np.arange(Q_BLOCK_W)[None, :]
    kstart = np.clip(np.arange(nqb) * Q_BLOCK_W - WIN_W // 2, 0, GRID_W - KEY_BLOCK_W)
    kcol = kstart[:, None] + np.arange(KEY_BLOCK_W)[None, :]
    wstart = np.clip(qcol - WIN_W // 2, 0, GRID_W - WIN_W)
    kc = kcol[:, None, :]
    valid = (kc >= wstart[..., None]) & (kc < wstart[..., None] + WIN_W)
    dc = np.clip(kc - qcol[..., None], -(WIN_W - 1), WIN_W - 1) + (WIN_W - 1)
    return kcol.astype(np.int32), valid, dc.astype(np.int32)


def neighbourhood_attention(q, k, v, kc, vc, rpb):
    B, S, H, Dh = q.shape
    rows = S // GRID_W
    kh = min(WIN_H, rows)
    nqb = GRID_W // Q_BLOCK_W
    scale = HEAD_DIM ** -0.5
    kcol, valid_np, dc_idx = column_tables()
    valid = jnp.asarray(valid_np)[:, :, None, :]
    bias_cols = rpb[:, :, dc_idx].astype(jnp.float32)
    q_rows = q.reshape(B, rows, nqb, Q_BLOCK_W, H, Dh).transpose(1, 0, 2, 3, 4, 5)
    k_grid = k.reshape(B, rows, GRID_W, H, Dh)
    v_grid = v.reshape(B, rows, GRID_W, H, Dh)
    n_loc = kh * KEY_BLOCK_W

    def row_block(args):
        i, qi = args
        r0 = jnp.clip(i - kh // 2, 0, rows - kh)
        kb = lax.dynamic_slice_in_dim(k_grid, r0, kh, axis=1)[:, :, kcol]
        vb = lax.dynamic_slice_in_dim(v_grid, r0, kh, axis=1)[:, :, kcol]
        dr_idx = r0 + jnp.arange(kh) - i + (WIN_H - 1)
        bias = jnp.take(bias_cols, dr_idx, axis=1).transpose(0, 2, 3, 1, 4)
        s_loc = jnp.einsum('bmqhd,brmchd->bhmqrc', qi, kb).astype(jnp.float32) * scale + bias[None]
        s_loc = jnp.where(valid, s_loc, -jnp.inf)
        s_ctx = jnp.einsum('bmqhd,blhd->bhmql', qi, kc).astype(jnp.float32) * scale
        s = jnp.concatenate([s_loc.reshape(B, H, nqb, Q_BLOCK_W, n_loc), s_ctx], axis=-1)
        p = jax.nn.softmax(s, axis=-1).astype(v.dtype)
        p_loc = p[..., :n_loc].reshape(B, H, nqb, Q_BLOCK_W, kh, KEY_BLOCK_W)
        p_ctx = p[..., n_loc:]
        return (jnp.einsum('bhmqrc,brmchd->bmqhd', p_loc, vb)
                + jnp.einsum('bhmql,blhd->bmqhd', p_ctx, vc))

    o = lax.map(row_block, (jnp.arange(rows), q_rows))
    return o.transpose(1, 0, 2, 3, 4, 5).reshape(B, S, H * Dh)


def context_attention(qc, kc, vc):
    B, L = qc.shape[0], qc.shape[1]
    s = jnp.einsum('bqhd,bkhd->bhqk', qc, kc).astype(jnp.float32) * (HEAD_DIM ** -0.5)
    p = jax.nn.softmax(s, axis=-1).astype(vc.dtype)
    return jnp.einsum('bhqk,bkhd->bqhd', p, vc).reshape(B, L, ATTN_W)


def conformer_conv(u, w_dw, b_dw, ln_g, ln_b, w_pw_out):
    a, g = jnp.split(u, 2, axis=-1)
    h = a * jax.nn.sigmoid(g)
    h = lax.conv_general_dilated(h, w_dw[:, None, :].astype(h.dtype), window_strides=(1,),
                                 padding=[(CONV_K // 2, CONV_K // 2)],
                                 dimension_numbers=('NWC', 'WIO', 'NWC'),
                                 feature_group_count=CONV_W)
    h = jax.nn.silu(layer_norm(h + b_dw, ln_g, ln_b))
    return h @ w_pw_out


def merge_branches(o_attn, u, ga, gc, w_attn_out, w_dw, b_dw, ln_g, ln_b, w_conv_out, w_out):
    y_attn = o_attn @ w_attn_out
    y_conv = conformer_conv(u, w_dw, b_dw, ln_g, ln_b, w_conv_out)
    return (jax.nn.sigmoid(ga) * y_attn + jax.nn.sigmoid(gc) * y_conv) @ w_out


def swiglu(h, wg, wu, wd):
    return (jax.nn.silu(h @ wg) * (h @ wu)) @ wd


def moe_swiglu(h, w_router, wg, wu, wd):
    lead = h.shape[:-1]
    t = h.reshape(-1, h.shape[-1])
    logits = (t @ w_router).astype(jnp.float32)
    top_v, top_i = lax.top_k(logits, TOP_K)
    top_w = jax.nn.softmax(top_v, axis=-1)
    gates = jnp.sum(jax.nn.one_hot(top_i, N_EXPERTS, dtype=jnp.float32) * top_w[..., None], axis=1)
    gates = gates.astype(t.dtype)
    out = jnp.zeros_like(t)
    for e in range(N_EXPERTS):
        out = out + gates[:, e:e + 1] * swiglu(t, wg[e], wu[e], wd[e])
    return out.reshape(lead + (h.shape[-1],))


def setup_inputs(seed: int = 0) -> dict:
    key = jax.random.key(seed)
    ks = jax.random.split(key, 32)
    f32 = jnp.float32

    def nrm(k, shape, fan_in, mult=1.0):
        return jax.random.normal(k, shape, f32) * (mult * fan_in ** -0.5)

    def gain(k, shape):
        return 1.0 + 0.02 * jax.random.normal(k, shape, f32)

    def small(k, shape, s=0.02):
        return s * jax.random.normal(k, shape, f32)

    D = D_MODEL
    return {
        "x": jax.random.normal(ks[0], (BATCH, SEQ, D), f32),
        "c": jax.random.normal(ks[1], (BATCH, D), f32),
        "ctx": jax.random.normal(ks[2], (BATCH, CTX_LEN, D), f32),
        "c_ctx": jax.random.normal(ks[3], (D,), f32),
        "w_mod": nrm(ks[4], (DEPTH, D, N_MOD * D), D, 0.5),
        "b_mod": small(ks[5], (DEPTH, N_MOD * D)),
        "g_mix_pre": gain(ks[6], (DEPTH, D)),
        "g_mix_post": gain(ks[7], (DEPTH, D)),
        "g_ffn_pre": gain(ks[8], (DEPTH, D)),
        "g_ffn_post": gain(ks[9], (DEPTH, D)),
        "w_in": nrm(ks[10], (DEPTH, D, IN_COLS), D),
        "rpb": small(ks[11], (DEPTH, N_HEADS, 2 * WIN_H - 1, 2 * WIN_W - 1), 0.1),
        "w_attn_out": nrm(ks[12], (DEPTH, ATTN_W, D), ATTN_W),
        "conv_dw": nrm(ks[13], (DEPTH, CONV_K, CONV_W), CONV_K),
        "conv_db": small(ks[14], (DEPTH, CONV_W)),
        "conv_ln_g": gain(ks[15], (DEPTH, CONV_W)),
        "conv_ln_b": small(ks[16], (DEPTH, CONV_W)),
        "w_conv_out": nrm(ks[17], (DEPTH, CONV_W, D), CONV_W),
        "w_out": nrm(ks[18], (DEPTH, D, D), D),
        "w_ff_gate": nrm(ks[19], (N_DENSE, D, D_FF), D),
        "w_ff_up": nrm(ks[20], (N_DENSE, D, D_FF), D),
        "w_ff_down": nrm(ks[21], (N_DENSE, D_FF, D), D_FF),
        "w_router": nrm(ks[22], (N_MOE, D, N_EXPERTS), D),
        "w_exp_gate": nrm(ks[23], (N_MOE, N_EXPERTS, D, D_FF_EXPERT), D),
        "w_exp_up": nrm(ks[24], (N_MOE, N_EXPERTS, D, D_FF_EXPERT), D),
        "w_exp_down": nrm(ks[25], (N_MOE, N_EXPERTS, D_FF_EXPERT, D), D_FF_EXPERT),
    }


def reference(x, c, ctx, c_ctx, w_mod, b_mod, g_mix_pre, g_mix_post, g_ffn_pre, g_ffn_post,
              w_in, rpb, w_attn_out, conv_dw, conv_db, conv_ln_g, conv_ln_b, w_conv_out, w_out,
              w_ff_gate, w_ff_up, w_ff_down, w_router, w_exp_gate, w_exp_up, w_exp_down):
    xc = ctx
    silu_c = jax.nn.silu(c)
    silu_cc = jax.nn.silu(c_ctx)
    for layer in range(DEPTH):
        last = layer == DEPTH - 1
        mod_x = (silu_c @ w_mod[layer] + b_mod[layer])[:, None, :]
        mod_c = (silu_cc @ w_mod[layer] + b_mod[layer])[None, None, :]
        shm_x, scm_x, gm_x, shf_x, scf_x, gf_x = jnp.split(mod_x, N_MOD, axis=-1)
        shm_c, scm_c, gm_c, shf_c, scf_c, gf_c = jnp.split(mod_c, N_MOD, axis=-1)
        branch_p = (w_attn_out[layer], conv_dw[layer], conv_db[layer], conv_ln_g[layer],
                    conv_ln_b[layer], w_conv_out[layer], w_out[layer])

        hx = modulate(rms_norm(x, g_mix_pre[layer]), shm_x, scm_x)
        hc = modulate(rms_norm(xc, g_mix_pre[layer]), shm_c, scm_c)
        qx, kx, vx, ux, gax, gcx = jnp.split(hx @ w_in[layer], IN_SPLITS, axis=-1)
        if last:
            kc, vc = jnp.split(hc @ w_in[layer][:, ATTN_W:3 * ATTN_W], 2, axis=-1)
        else:
            qc, kc, vc, uc, gac, gcc = jnp.split(hc @ w_in[layer], IN_SPLITS, axis=-1)
        kc_h, vc_h = split_heads(kc), split_heads(vc)
        ox = neighbourhood_attention(split_heads(qx), split_heads(kx), split_heads(vx), kc_h, vc_h, rpb[layer])
        yx = merge_branches(ox, ux, gax, gcx, *branch_p)
        x = x + gm_x * rms_norm(yx, g_mix_post[layer])
        if not last:
            oc = context_attention(split_heads(qc), kc_h, vc_h)
            yc = merge_branches(oc, uc, gac, gcc, *branch_p)
            xc = xc + gm_c * rms_norm(yc, g_mix_post[layer])

        j = layer // 2
        fx = modulate(rms_norm(x, g_ffn_pre[layer]), shf_x, scf_x)
        if layer % 2 == 0:
            fx = swiglu(fx, w_ff_gate[j], w_ff_up[j], w_ff_down[j])
        else:
            fx = moe_swiglu(fx, w_router[j], w_exp_gate[j], w_exp_up[j], w_exp_down[j])
        x = x + gf_x * rms_norm(fx, g_ffn_post[layer])
        if not last:
            fc = modulate(rms_norm(xc, g_ffn_pre[layer]), shf_c, scf_c)
            if layer % 2 == 0:
                fc = swiglu(fc, w_ff_gate[j], w_ff_up[j], w_ff_down[j])
            else:
                fc = moe_swiglu(fc, w_router[j], w_exp_gate[j], w_exp_up[j], w_exp_down[j])
            xc = xc + gf_c * rms_norm(fc, g_ffn_post[layer])
    return x
```

```python
import functools

import numpy as np
import jax
import jax.numpy as jnp
from jax import lax
from jax.experimental import pallas as pl
from jax.experimental.pallas import tpu as pltpu

F32 = jnp.float32
BF16 = jnp.bfloat16

EPS = 1e-6
GRID_W = 64
N_HEADS = 16
HEAD_DIM = 128
CONV_K = 31
WIN_H = 8
WIN_W = 16
N_MOD = 6
N_EXPERTS = 8
MOD_ROWS = 8

ATT_QROWS = 4
ATT_KROWS = 12
CONV_T = 256
CONV_HALO = 16
MOE_TM = 512

VMEM_LIMIT = 56 * 1024 * 1024


def _cparams(sem):
    return pltpu.CompilerParams(dimension_semantics=sem, vmem_limit_bytes=VMEM_LIMIT)


def _silu(x):
    return x * jax.nn.sigmoid(x)


def _rms(x, g):
    return x * lax.rsqrt(jnp.mean(x * x, axis=-1, keepdims=True) + EPS) * g


def _mod_row(mod_ref, grp, col, d):
    return mod_ref[pl.ds(grp, 1), col * d:(col + 1) * d]


def _row_chunks(n_rows, rows, body):
    def step(r, carry):
        body(pl.multiple_of(r * rows, rows))
        return carry
    lax.fori_loop(0, n_rows // rows, step, 0)


def _prenorm_chunk(x, g, shift, scale1):
    return _rms(x, g) * scale1 + shift


def _prenorm_into(h_ref, x_ref, g_ref, mod_ref, grp, shift_col, scale_col, rows=128):
    d = x_ref.shape[-1]
    shift = _mod_row(mod_ref, grp, shift_col, d)
    scale1 = 1.0 + _mod_row(mod_ref, grp, scale_col, d)
    g = g_ref[...]

    def body(r0):
        x = x_ref[pl.ds(r0, rows), :]
        h_ref[pl.ds(r0, rows), :] = _prenorm_chunk(x, g, shift, scale1).astype(h_ref.dtype)

    _row_chunks(x_ref.shape[0], rows, body)


def _mod_kernel(s_ref, w_ref, b_ref, o_ref):
    o_ref[...] = jnp.dot(s_ref[...], w_ref[...], preferred_element_type=F32,
                         precision=lax.Precision.HIGHEST) + b_ref[...]


def _mod_table(s, w_mod, b_mod):
    depth, d, n = w_mod.shape
    tn = 1024
    return pl.pallas_call(
        _mod_kernel,
        out_shape=jax.ShapeDtypeStruct((depth, MOD_ROWS, n), F32),
        grid=(depth, n // tn),
        in_specs=[
            pl.BlockSpec((MOD_ROWS, d), lambda l, j: (0, 0)),
            pl.BlockSpec((None, d, tn), lambda l, j: (l, 0, j)),
            pl.BlockSpec((None, 1, tn), lambda l, j: (l, 0, j)),
        ],
        out_specs=pl.BlockSpec((None, MOD_ROWS, tn), lambda l, j: (l, 0, j)),
        compiler_params=_cparams(("parallel", "parallel")),
        name="mod_table",
    )(s, w_mod, b_mod.reshape(depth, 1, n))


def _in_proj_kernel(x_ref, g_ref, mod_ref, w_ref, o_ref, h_ref, *, tiles_per_batch, n_batch):
    i = pl.program_id(0)

    @pl.when(pl.program_id(1) == 0)
    def _():
        grp = jnp.minimum(i // tiles_per_batch, n_batch)
        _prenorm_into(h_ref, x_ref, g_ref, mod_ref, grp, 0, 1)

    o_ref[...] = jnp.dot(h_ref[...], w_ref[...], preferred_element_type=F32).astype(o_ref.dtype)


def _in_proj(xs, g, mod, w, layer, *, n_batch, seq, tm=1024, tn=1024):
    m, d = xs.shape
    n = w.shape[-1]
    kern = functools.partial(_in_proj_kernel, tiles_per_batch=seq // tm, n_batch=n_batch)
    return pl.pallas_call(
        kern,
        out_shape=jax.ShapeDtypeStruct((m, n), BF16),
        grid=(m // tm, n // tn),
        in_specs=[
            pl.BlockSpec((tm, d), lambda i, j: (i, 0)),
            pl.BlockSpec((None, 1, d), lambda i, j: (layer, 0, 0)),
            pl.BlockSpec((None, MOD_ROWS, N_MOD * d), lambda i, j: (layer, 0, 0)),
            pl.BlockSpec((None, d, tn), lambda i, j: (layer, 0, j)),
        ],
        out_specs=pl.BlockSpec((tm, tn), lambda i, j: (i, j)),
        scratch_shapes=[pltpu.VMEM((tm, d), BF16)],
        compiler_params=_cparams(("parallel", "arbitrary")),
        name="in_proj",
    )(xs, g, mod, w)


def _attn_tables():
    rows = GRID_W
    nq = ATT_QROWS * GRID_W
    nk = ATT_KROWS * GRID_W
    q = np.arange(nq)
    key = np.arange(nk)
    qi, qj = q // GRID_W, q % GRID_W
    kr, kc = key // GRID_W, key % GRID_W
    drs, valids = [], []
    for i0, start in ((0, 0), (2 * ATT_QROWS, ATT_QROWS), (rows - ATT_QROWS, rows - ATT_KROWS)):
        i = (i0 + qi)[:, None]
        r = (start + kr)[None, :]
        r0 = np.clip(i - WIN_H // 2, 0, rows - WIN_H)
        row_ok = (r >= r0) & (r < r0 + WIN_H)
        ws = np.clip(qj - WIN_W // 2, 0, GRID_W - WIN_W)[:, None]
        col_ok = (kc[None, :] >= ws) & (kc[None, :] < ws + WIN_W)
        drs.append(np.clip(r - i + (WIN_H - 1), 0, 2 * WIN_H - 2) + 0 * kc[None, :])
        valids.append(row_ok & col_ok)
    dc = np.clip(kc[None, :] - qj[:, None], -(WIN_W - 1), WIN_W - 1) + (WIN_W - 1)
    return np.stack(drs).astype(np.int32), dc.astype(np.int32), np.stack(valids)


def _attn_bias_tiles(rpb):
    dr, dc, valid = _attn_tables()
    b = rpb[:, :, dr, dc[None]]
    b = jnp.where(valid[None, None], b.astype(F32), -jnp.inf)
    none = jnp.full(b.shape[:2] + (1,) + b.shape[3:], -jnp.inf, F32)
    return jnp.concatenate([b, none], axis=2)


def _attn_kernel(q_ref, k_ref, v_ref, kc_ref, vc_ref, bias_ref, o_ref, *, n_qblocks, max_start):
    rb = pl.program_id(2)
    start = jnp.clip(rb * ATT_QROWS - WIN_H // 2, 0, max_start)
    start = jnp.where(rb == n_qblocks, 0, start)
    tok0 = pl.multiple_of(start * GRID_W, GRID_W)
    nk = ATT_KROWS * GRID_W
    q = q_ref[...]
    kw = k_ref[pl.ds(tok0, nk), :]
    vw = v_ref[pl.ds(tok0, nk), :]
    scale = HEAD_DIM ** -0.5
    dn = (((1,), (1,)), ((), ()))
    s_loc = lax.dot_general(q, kw, dn, preferred_element_type=F32) * scale + bias_ref[...]
    s_ctx = lax.dot_general(q, kc_ref[...], dn, preferred_element_type=F32) * scale
    m = jnp.maximum(jnp.max(s_loc, axis=-1, keepdims=True), jnp.max(s_ctx, axis=-1, keepdims=True))
    p_loc = jnp.exp(s_loc - m)
    p_ctx = jnp.exp(s_ctx - m)
    l = jnp.sum(p_loc, axis=-1, keepdims=True) + jnp.sum(p_ctx, axis=-1, keepdims=True)
    o = (jnp.dot(p_loc.astype(BF16), vw, preferred_element_type=F32)
         + jnp.dot(p_ctx.astype(BF16), vc_ref[...], preferred_element_type=F32))
    o_ref[...] = (o / l).astype(o_ref.dtype)


def _attention(proj, bias, layer, *, n_batch, seq, ctx_len, with_ctx_queries):
    m = proj.shape[0]
    nq = ATT_QROWS * GRID_W
    assert nq == ctx_len, "context queries reuse the latent query-block shape"
    rows = seq // GRID_W
    n_qblocks = rows // ATT_QROWS
    ctx_blk0 = (n_batch * seq) // ctx_len
    h_ = N_HEADS

    def qrow(b, rb):
        return jnp.where(rb == n_qblocks, ctx_blk0 + b, b * n_qblocks + rb)

    def pattern(rb):
        return jnp.where(rb == 0, 0, jnp.where(rb == n_qblocks - 1, 2, jnp.where(rb == n_qblocks, 3, 1)))

    kern = functools.partial(_attn_kernel, n_qblocks=n_qblocks, max_start=rows - ATT_KROWS)
    return pl.pallas_call(
        kern,
        out_shape=jax.ShapeDtypeStruct((m, h_ * HEAD_DIM), BF16),
        grid=(n_batch, h_, n_qblocks + (1 if with_ctx_queries else 0)),
        in_specs=[
            pl.BlockSpec((nq, HEAD_DIM), lambda b, h, rb: (qrow(b, rb), h)),
            pl.BlockSpec((seq, HEAD_DIM), lambda b, h, rb: (b, h_ + h)),
            pl.BlockSpec((seq, HEAD_DIM), lambda b, h, rb: (b, 2 * h_ + h)),
            pl.BlockSpec((ctx_len, HEAD_DIM), lambda b, h, rb: (ctx_blk0 + b, h_ + h)),
            pl.BlockSpec((ctx_len, HEAD_DIM), lambda b, h, rb: (ctx_blk0 + b, 2 * h_ + h)),
            pl.BlockSpec((None, None, None, nq, ATT_KROWS * GRID_W),
                         lambda b, h, rb: (layer, h, pattern(rb), 0, 0)),
        ],
        out_specs=pl.BlockSpec((nq, HEAD_DIM), lambda b, h, rb: (qrow(b, rb), h)),
        compiler_params=_cparams(("parallel", "parallel", "arbitrary")),
        name="attention",
    )(proj, proj, proj, proj, proj, bias)


def _conv_kernel(a_ref, g_ref, ap_ref, gp_ref, an_ref, gn_ref, w_ref, b_ref, lg_ref, lb_ref,
                 o_ref, hext_ref, y_ref, *, tiles_per_seq, n_latent_tiles):
    i = pl.program_id(0)
    t = CONV_T
    halo = CONV_HALO
    c = a_ref.shape[-1]
    is_ctx = i >= n_latent_tiles
    pos = i % tiles_per_seq
    at_start = jnp.logical_or(is_ctx, pos == 0)
    at_end = jnp.logical_or(is_ctx, pos == tiles_per_seq - 1)

    def glu(a, g):
        return a.astype(F32) * jax.nn.sigmoid(g.astype(F32))

    lanes = 128
    rchunk = 64
    off = halo - CONV_K // 2

    hext_ref[0:halo, :] = jnp.where(at_start, 0.0, glu(ap_ref[...], gp_ref[...]))
    hext_ref[halo + t:halo + t + halo, :] = jnp.where(at_end, 0.0, glu(an_ref[...], gn_ref[...]))

    def glu_body(r0):
        dst = pl.multiple_of(r0 + halo, halo)
        hext_ref[pl.ds(dst, rchunk), :] = glu(a_ref[pl.ds(r0, rchunk), :], g_ref[pl.ds(r0, rchunk), :])

    _row_chunks(t, rchunk, glu_body)

    def chunk_body(ci, carry):
        c0 = pl.multiple_of(ci * lanes, lanes)
        wts = w_ref[:, pl.ds(c0, lanes)]
        bias = b_ref[:, pl.ds(c0, lanes)]
        for r0 in range(0, t, rchunk):
            acc = jnp.zeros((rchunk, lanes), F32)
            for k in range(CONV_K):
                acc = acc + wts[k:k + 1, :] * hext_ref[r0 + k + off:r0 + k + off + rchunk, pl.ds(c0, lanes)]
            y_ref[r0:r0 + rchunk, pl.ds(c0, lanes)] = acc + bias
        return carry

    lax.fori_loop(0, c // lanes, chunk_body, 0)

    ln_g = lg_ref[...]
    ln_b = lb_ref[...]

    def ln_body(r0):
        y = y_ref[pl.ds(r0, rchunk), :]
        mu = jnp.mean(y, axis=-1, keepdims=True)
        yc = y - mu
        var = jnp.mean(yc * yc, axis=-1, keepdims=True)
        z = yc * lax.rsqrt(var + EPS) * ln_g + ln_b
        o_ref[pl.ds(r0, rchunk), :] = _silu(z).astype(o_ref.dtype)

    _row_chunks(t, rchunk, ln_body)


def _conv_module(proj, w_dw, b_dw, ln_g, ln_b, layer, *, n_batch, seq, ctx_len, a_col, g_col):
    m = proj.shape[0]
    c = w_dw.shape[-1]
    t = CONV_T
    assert ctx_len == t and seq % t == 0
    hb = t // CONV_HALO
    last_hblk = m // CONV_HALO - 1
    n_tiles = m // t
    kern = functools.partial(_conv_kernel, tiles_per_seq=seq // t, n_latent_tiles=(n_batch * seq) // t)

    def prev(i):
        return jnp.maximum(i * hb - 1, 0)

    def nxt(i):
        return jnp.minimum(i * hb + hb, last_hblk)

    vec = lambda: pl.BlockSpec((None, 1, c), lambda i: (layer, 0, 0))
    return pl.pallas_call(
        kern,
        out_shape=jax.ShapeDtypeStruct((m, c), BF16),
        grid=(n_tiles,),
        in_specs=[
            pl.BlockSpec((t, c), lambda i: (i, a_col)),
            pl.BlockSpec((t, c), lambda i: (i, g_col)),
            pl.BlockSpec((CONV_HALO, c), lambda i: (prev(i), a_col)),
            pl.BlockSpec((CONV_HALO, c), lambda i: (prev(i), g_col)),
            pl.BlockSpec((CONV_HALO, c), lambda i: (nxt(i), a_col)),
            pl.BlockSpec((CONV_HALO, c), lambda i: (nxt(i), g_col)),
            pl.BlockSpec((None, CONV_K, c), lambda i: (layer, 0, 0)),
            vec(), vec(), vec(),
        ],
        out_specs=pl.BlockSpec((t, c), lambda i: (i, 0)),
        scratch_shapes=[pltpu.VMEM((t + 2 * CONV_HALO, c), F32), pltpu.VMEM((t, c), F32)],
        compiler_params=_cparams(("parallel",)),
        name="conv_module",
    )(proj, proj, proj, proj, proj, proj, w_dw, b_dw, ln_g, ln_b)


def _merge_kernel(oa_ref, hc_ref, wa_ref, wc_ref, ga_ref, gc_ref, o_ref):
    ya = jnp.dot(oa_ref[...], wa_ref[...], preferred_element_type=F32)
    yc = jnp.dot(hc_ref[...], wc_ref[...], preferred_element_type=F32)
    y = jax.nn.sigmoid(ga_ref[...].astype(F32)) * ya + jax.nn.sigmoid(gc_ref[...].astype(F32)) * yc
    o_ref[...] = y.astype(o_ref.dtype)


def _merge(o_attn, h_conv, wa, wc, proj, layer, *, ga_col0, gc_col0, tm=512, tn=1024):
    m, k = o_attn.shape
    n = wa.shape[-1]
    return pl.pallas_call(
        _merge_kernel,
        out_shape=jax.ShapeDtypeStruct((m, n), BF16),
        grid=(n // tn, m // tm),
        in_specs=[
            pl.BlockSpec((tm, k), lambda j, i: (i, 0)),
            pl.BlockSpec((tm, k), lambda j, i: (i, 0)),
            pl.BlockSpec((None, k, tn), lambda j, i: (layer, 0, j)),
            pl.BlockSpec((None, k, tn), lambda j, i: (layer, 0, j)),
            pl.BlockSpec((tm, tn), lambda j, i: (i, ga_col0 // tn + j)),
            pl.BlockSpec((tm, tn), lambda j, i: (i, gc_col0 // tn + j)),
        ],
        out_specs=pl.BlockSpec((tm, tn), lambda j, i: (i, j)),
        compiler_params=_cparams(("parallel", "parallel")),
        name="merge",
    )(o_attn, h_conv, wa, wc, proj, proj)


def _post_residual(z_chunk, x_ref, g_ref, mod_ref, grp, gate_col, o_ref, rows=128):
    d = x_ref.shape[-1]
    gate = _mod_row(mod_ref, grp, gate_col, d)
    g = g_ref[...]

    def body(r0):
        o_ref[pl.ds(r0, rows), :] = x_ref[pl.ds(r0, rows), :] + gate * _rms(z_chunk(r0), g)

    _row_chunks(x_ref.shape[0], rows, body)


def _matmul_res_kernel(a_ref, w_ref, x_ref, g_ref, mod_ref, o_ref, acc_ref, *,
                       tiles_per_batch, n_batch, gate_col, nk):
    i = pl.program_id(0)
    kk = pl.program_id(1)
    part = jnp.dot(a_ref[...], w_ref[...], preferred_element_type=F32)

    @pl.when(kk == 0)
    def _():
        acc_ref[...] = part

    @pl.when(kk > 0)
    def _():
        acc_ref[...] += part

    @pl.when(kk == nk - 1)
    def _():
        grp = jnp.minimum(i // tiles_per_batch, n_batch)
        rows = 128
        _post_residual(lambda r0: acc_ref[pl.ds(r0, rows), :], x_ref, g_ref, mod_ref, grp, gate_col,
                       o_ref, rows)


def _matmul_residual(a, w, xs, g, mod, layer, *, gate_col, n_batch, seq, m_rows, w_index, tm=512, tk=None):
    k = a.shape[1]
    d = xs.shape[1]
    tk = k if tk is None else tk
    nk = k // tk
    kern = functools.partial(_matmul_res_kernel, tiles_per_batch=seq // tm, n_batch=n_batch,
                             gate_col=gate_col, nk=nk)
    return pl.pallas_call(
        kern,
        out_shape=jax.ShapeDtypeStruct(xs.shape, F32),
        grid=(m_rows // tm, nk),
        in_specs=[
            pl.BlockSpec((tm, tk), lambda i, kk: (i, kk)),
            pl.BlockSpec((None, tk, d), lambda i, kk: (w_index, kk, 0)),
            pl.BlockSpec((tm, d), lambda i, kk: (i, 0)),
            pl.BlockSpec((None, 1, d), lambda i, kk: (layer, 0, 0)),
            pl.BlockSpec((None, MOD_ROWS, N_MOD * d), lambda i, kk: (layer, 0, 0)),
        ],
        out_specs=pl.BlockSpec((tm, d), lambda i, kk: (i, 0)),
        scratch_shapes=[pltpu.VMEM((tm, d), F32)],
        input_output_aliases={2: 0},
        compiler_params=_cparams(("parallel", "arbitrary")),
        name="matmul_residual",
    )(a, w, xs, g, mod)


def _ffn_in_kernel(x_ref, g_ref, mod_ref, wg_ref, wu_ref, o_ref, h_ref, *, tiles_per_batch, n_batch):
    i = pl.program_id(0)

    @pl.when(pl.program_id(1) == 0)
    def _():
        grp = jnp.minimum(i // tiles_per_batch, n_batch)
        _prenorm_into(h_ref, x_ref, g_ref, mod_ref, grp, 3, 4)

    h = h_ref[...]
    a = jnp.dot(h, wg_ref[...], preferred_element_type=F32)
    u = jnp.dot(h, wu_ref[...], preferred_element_type=F32)
    o_ref[...] = (_silu(a) * u).astype(o_ref.dtype)


def _ffn_in(xs, g, mod, wg, wu, layer, j_dense, *, n_batch, seq, m_rows, tm=1024, tn=512):
    d = xs.shape[1]
    f = wg.shape[-1]
    kern = functools.partial(_ffn_in_kernel, tiles_per_batch=seq // tm, n_batch=n_batch)
    return pl.pallas_call(
        kern,
        out_shape=jax.ShapeDtypeStruct((m_rows, f), BF16),
        grid=(m_rows // tm, f // tn),
        in_specs=[
            pl.BlockSpec((tm, d), lambda i, j: (i, 0)),
            pl.BlockSpec((None, 1, d), lambda i, j: (layer, 0, 0)),
            pl.BlockSpec((None, MOD_ROWS, N_MOD * d), lambda i, j: (layer, 0, 0)),
            pl.BlockSpec((None, d, tn), lambda i, j: (j_dense, 0, j)),
            pl.BlockSpec((None, d, tn), lambda i, j: (j_dense, 0, j)),
        ],
        out_specs=pl.BlockSpec((tm, tn), lambda i, j: (i, j)),
        scratch_shapes=[pltpu.VMEM((tm, d), BF16)],
        compiler_params=_cparams(("parallel", "arbitrary")),
        name="ffn_in",
    )(xs, g, mod, wg, wu)


def _router_kernel(x_ref, g_ref, mod_ref, wr_ref, h_ref, ti_ref, tw_ref, *, tiles_per_batch, n_batch):
    i = pl.program_id(0)
    grp = jnp.minimum(i // tiles_per_batch, n_batch)
    d = x_ref.shape[-1]
    shift = _mod_row(mod_ref, grp, 3, d)
    scale1 = 1.0 + _mod_row(mod_ref, grp, 4, d)
    g = g_ref[...]
    rows = 128

    def body(r0):
        h = _prenorm_chunk(x_ref[pl.ds(r0, rows), :], g, shift, scale1)
        h_ref[pl.ds(r0, rows), :] = h.astype(h_ref.dtype)
        logits = jnp.dot(h, wr_ref[...], preferred_element_type=F32, precision=lax.Precision.HIGHEST)
        lane = lax.broadcasted_iota(jnp.int32, logits.shape, 1)
        big = jnp.int32(logits.shape[-1])
        logits = jnp.where(lane < N_EXPERTS, logits, -jnp.inf)
        m1 = jnp.max(logits, axis=-1, keepdims=True)
        i1 = jnp.min(jnp.where(logits == m1, lane, big), axis=-1, keepdims=True)
        rest = jnp.where(lane == i1, -jnp.inf, logits)
        m2 = jnp.max(rest, axis=-1, keepdims=True)
        i2 = jnp.min(jnp.where(rest == m2, lane, big), axis=-1, keepdims=True)
        e2 = jnp.exp(m2 - m1)
        w1 = 1.0 / (1.0 + e2)
        w2 = e2 / (1.0 + e2)
        ti_ref[pl.ds(r0, rows), :] = jnp.where(lane == 0, i1, jnp.where(lane == 1, i2, 0))
        tw_ref[pl.ds(r0, rows), :] = jnp.where(lane == 0, w1, jnp.where(lane == 1, w2, 0.0))

    _row_chunks(x_ref.shape[0], rows, body)


def _router(xs, g, mod, w_router_pad, layer, j_moe, *, n_batch, seq, m_rows, tm=512):
    d = xs.shape[1]
    lanes = w_router_pad.shape[-1]
    kern = functools.partial(_router_kernel, tiles_per_batch=seq // tm, n_batch=n_batch)
    return pl.pallas_call(
        kern,
        out_shape=(jax.ShapeDtypeStruct((m_rows, d), BF16),
                   jax.ShapeDtypeStruct((m_rows, lanes), jnp.int32),
                   jax.ShapeDtypeStruct((m_rows, lanes), F32)),
        grid=(m_rows // tm,),
        in_specs=[
            pl.BlockSpec((tm, d), lambda i: (i, 0)),
            pl.BlockSpec((None, 1, d), lambda i: (layer, 0, 0)),
            pl.BlockSpec((None, MOD_ROWS, N_MOD * d), lambda i: (layer, 0, 0)),
            pl.BlockSpec((None, d, lanes), lambda i: (j_moe, 0, 0)),
        ],
        out_specs=(pl.BlockSpec((tm, d), lambda i: (i, 0)),
                   pl.BlockSpec((tm, lanes), lambda i: (i, 0)),
                   pl.BlockSpec((tm, lanes), lambda i: (i, 0))),
        compiler_params=_cparams(("parallel",)),
        name="router",
    )(xs, g, mod, w_router_pad)


def _moe_up_kernel(te_ref, nu_ref, x_ref, wg_ref, wu_ref, o_ref):
    t = pl.program_id(1)

    @pl.when(t < nu_ref[0])
    def _():
        x = x_ref[...]
        a = jnp.dot(x, wg_ref[...], preferred_element_type=F32)
        u = jnp.dot(x, wu_ref[...], preferred_element_type=F32)
        o_ref[...] = (_silu(a) * u).astype(o_ref.dtype)

    @pl.when(t >= nu_ref[0])
    def _():
        o_ref[...] = jnp.zeros_like(o_ref)


def _moe_up(tile_expert, n_used, x_sorted, wg, wu, j_moe, *, tn=1024):
    r, d = x_sorted.shape
    f = wg.shape[-1]
    tm = MOE_TM
    grid_spec = pltpu.PrefetchScalarGridSpec(
        num_scalar_prefetch=2,
        grid=(f // tn, r // tm),
        in_specs=[
            pl.BlockSpec((tm, d), lambda j, t, te, nu: (t, 0)),
            pl.BlockSpec((None, None, d, tn), lambda j, t, te, nu: (j_moe, te[t], 0, j)),
            pl.BlockSpec((None, None, d, tn), lambda j, t, te, nu: (j_moe, te[t], 0, j)),
        ],
        out_specs=pl.BlockSpec((tm, tn), lambda j, t, te, nu: (t, j)),
    )
    return pl.pallas_call(
        _moe_up_kernel,
        out_shape=jax.ShapeDtypeStruct((r, f), BF16),
        grid_spec=grid_spec,
        compiler_params=_cparams(("parallel", "arbitrary")),
        name="moe_up",
    )(tile_expert, n_used, x_sorted, wg, wu)


def _moe_down_kernel(te_ref, nu_ref, a_ref, wd_ref, rw_ref, o_ref):
    t = pl.program_id(1)

    @pl.when(t < nu_ref[0])
    def _():
        y = jnp.dot(a_ref[...], wd_ref[...], preferred_element_type=F32)
        o_ref[...] = (y * rw_ref[...]).astype(o_ref.dtype)

    @pl.when(t >= nu_ref[0])
    def _():
        o_ref[...] = jnp.zeros_like(o_ref)


def _moe_down(tile_expert, n_used, act, wd, row_w, j_moe, *, tn=1024):
    r, f = act.shape
    d = wd.shape[-1]
    tm = MOE_TM
    grid_spec = pltpu.PrefetchScalarGridSpec(
        num_scalar_prefetch=2,
        grid=(d // tn, r // tm),
        in_specs=[
            pl.BlockSpec((tm, f), lambda j, t, te, nu: (t, 0)),
            pl.BlockSpec((None, None, f, tn), lambda j, t, te, nu: (j_moe, te[t], 0, j)),
            pl.BlockSpec((tm, 1), lambda j, t, te, nu: (t, 0)),
        ],
        out_specs=pl.BlockSpec((tm, tn), lambda j, t, te, nu: (t, j)),
    )
    return pl.pallas_call(
        _moe_down_kernel,
        out_shape=jax.ShapeDtypeStruct((r, d), F32),
        grid_spec=grid_spec,
        compiler_params=_cparams(("parallel", "arbitrary")),
        name="moe_down",
    )(tile_expert, n_used, act, wd, row_w)


def _norm_res_kernel(y0_ref, y1_ref, x_ref, g_ref, mod_ref, o_ref, *, tiles_per_batch, n_batch, gate_col):
    i = pl.program_id(0)
    grp = jnp.minimum(i // tiles_per_batch, n_batch)
    rows = 128
    _post_residual(lambda r0: y0_ref[pl.ds(r0, rows), :] + y1_ref[pl.ds(r0, rows), :],
                   x_ref, g_ref, mod_ref, grp, gate_col, o_ref, rows)


def _norm_residual(y0, y1, xs, g, mod, layer, *, gate_col, n_batch, seq, m_rows, tm=512):
    d = xs.shape[1]
    kern = functools.partial(_norm_res_kernel, tiles_per_batch=seq // tm, n_batch=n_batch, gate_col=gate_col)
    row = lambda: pl.BlockSpec((tm, d), lambda i: (i, 0))
    return pl.pallas_call(
        kern,
        out_shape=jax.ShapeDtypeStruct(xs.shape, F32),
        grid=(m_rows // tm,),
        in_specs=[row(), row(), row(),
                  pl.BlockSpec((None, 1, d), lambda i: (layer, 0, 0)),
                  pl.BlockSpec((None, MOD_ROWS, N_MOD * d), lambda i: (layer, 0, 0))],
        out_specs=row(),
        input_output_aliases={2: 0},
        compiler_params=_cparams(("parallel",)),
        name="norm_residual",
    )(y0, y1, xs, g, mod)


def _route_tables(top_i, top_w, tm):
    n = top_i.shape[0]
    flat_e = top_i.reshape(-1)
    onehot = (flat_e[:, None] == jnp.arange(N_EXPERTS, dtype=jnp.int32)[None, :]).astype(jnp.int32)
    csum = jnp.cumsum(onehot, axis=0)
    rank = jnp.sum((csum - 1) * onehot, axis=1)
    counts = csum[-1]
    padded = ((counts + tm - 1) // tm) * tm
    ends = jnp.cumsum(padded)
    starts = ends - padded
    pos = starts[flat_e] + rank
    r_pad = 2 * n + N_EXPERTS * tm
    n_tiles = r_pad // tm
    row_token = jnp.zeros((r_pad,), jnp.int32).at[pos].set(jnp.arange(2 * n, dtype=jnp.int32) // 2)
    row_w = jnp.zeros((r_pad,), F32).at[pos].set(top_w.reshape(-1))
    n_used = (ends[-1] // tm).astype(jnp.int32)
    tile_start = jnp.arange(n_tiles, dtype=jnp.int32) * tm
    tile_expert = jnp.sum((tile_start[:, None] >= ends[None, :]).astype(jnp.int32), axis=1)
    last = jnp.minimum(tile_expert[jnp.maximum(n_used - 1, 0)], N_EXPERTS - 1)
    tile_expert = jnp.where(jnp.arange(n_tiles) < n_used, jnp.minimum(tile_expert, N_EXPERTS - 1), last)
    return pos.reshape(n, 2), row_token, row_w, tile_expert.astype(jnp.int32), n_used.reshape(1)


def kernel(x, c, ctx, c_ctx, w_mod, b_mod, g_mix_pre, g_mix_post, g_ffn_pre, g_ffn_post, w_in, rpb,
           w_attn_out, conv_dw, conv_db, conv_ln_g, conv_ln_b, w_conv_out, w_out, w_ff_gate, w_ff_up,
           w_ff_down, w_router, w_exp_gate, w_exp_up, w_exp_down):
    n_batch, seq, d = x.shape
    ctx_len = ctx.shape[1]
    depth = w_mod.shape[0]
    n_lat = n_batch * seq
    m_all = n_lat + n_batch * ctx_len
    assert seq == GRID_W * GRID_W and d == N_HEADS * HEAD_DIM and n_batch + 1 <= MOD_ROWS
    geo = dict(n_batch=n_batch, seq=seq)

    w_in_b = w_in.astype(BF16)
    w_ao_b = w_attn_out.astype(BF16)
    w_co_b = w_conv_out.astype(BF16)
    w_o_b = w_out.astype(BF16)
    w_fg_b, w_fu_b, w_fd_b = w_ff_gate.astype(BF16), w_ff_up.astype(BF16), w_ff_down.astype(BF16)
    w_eg_b, w_eu_b, w_ed_b = w_exp_gate.astype(BF16), w_exp_up.astype(BF16), w_exp_down.astype(BF16)
    w_router_pad = jnp.pad(w_router, ((0, 0), (0, 0), (0, 128 - N_EXPERTS)))

    vec = lambda a: a.reshape(depth, 1, -1)
    g_mix_pre, g_mix_post, g_ffn_pre, g_ffn_post = map(vec, (g_mix_pre, g_mix_post, g_ffn_pre, g_ffn_post))
    conv_db, conv_ln_g, conv_ln_b = map(vec, (conv_db, conv_ln_g, conv_ln_b))

    s = jnp.zeros((MOD_ROWS, d), F32).at[:n_batch].set(_silu(c)).at[n_batch].set(_silu(c_ctx))
    mod = _mod_table(s, w_mod, b_mod)
    bias = _attn_bias_tiles(rpb)

    xs = jnp.concatenate([x.reshape(n_lat, d), ctx.reshape(n_batch * ctx_len, d)], axis=0)
    attn_w = N_HEADS * HEAD_DIM
    conv_w = conv_dw.shape[-1]
    u_col0 = 3 * attn_w
    ga_col0 = u_col0 + 2 * conv_w
    gc_col0 = ga_col0 + d

    for layer in range(depth):
        last = layer == depth - 1
        m_rows = n_lat if last else m_all
        j = layer // 2

        proj = _in_proj(xs, g_mix_pre, mod, w_in_b, layer, **geo)
        o_attn = _attention(proj, bias, layer, ctx_len=ctx_len, with_ctx_queries=not last, **geo)
        h_conv = _conv_module(proj, conv_dw, conv_db, conv_ln_g, conv_ln_b, layer, ctx_len=ctx_len,
                              a_col=u_col0 // conv_w, g_col=u_col0 // conv_w + 1, **geo)
        y = _merge(o_attn, h_conv, w_ao_b, w_co_b, proj, layer, ga_col0=ga_col0, gc_col0=gc_col0)
        xs = _matmul_residual(y, w_o_b, xs, g_mix_post, mod, layer, gate_col=2, m_rows=m_rows,
                              w_index=layer, **geo)

        if layer % 2 == 0:
            act = _ffn_in(xs, g_ffn_pre, mod, w_fg_b, w_fu_b, layer, j, m_rows=m_rows, **geo)
            xs = _matmul_residual(act, w_fd_b, xs, g_ffn_post, mod, layer, gate_col=5, m_rows=m_rows,
                                  w_index=j, tk=1408, **geo)
        else:
            h, top_i, top_w = _router(xs, g_ffn_pre, mod, w_router_pad, layer, j, m_rows=m_rows, **geo)
            pos, row_token, row_w, tile_expert, n_used = _route_tables(top_i[:, :2], top_w[:, :2], MOE_TM)
            x_sorted = jnp.take(h, row_token, axis=0)
            act = _moe_up(tile_expert, n_used, x_sorted, w_eg_b, w_eu_b, j)
            y_sorted = _moe_down(tile_expert, n_used, act, w_ed_b, row_w[:, None], j)
            y0 = jnp.take(y_sorted, pos[:, 0], axis=0)
            y1 = jnp.take(y_sorted, pos[:, 1], axis=0)
            xs = _norm_residual(y0, y1, xs, g_ffn_post, mod, layer, gate_col=5, m_rows=m_rows, **geo)

    return xs[:n_lat].reshape(n_batch, seq, d)
```

```python
import functools

import numpy as np
import jax
import jax.numpy as jnp
from jax import lax
from jax.experimental import pallas as pl
from jax.experimental.pallas import tpu as pltpu

F32 = jnp.float32
BF16 = jnp.bfloat16

EPS = 1e-6
GRID_W = 64
N_HEADS = 16
HEAD_DIM = 128
CONV_K = 31
WIN_H = 8
WIN_W = 16
N_MOD = 6
N_EXPERTS = 8
MOD_ROWS = 8

ATT_QROWS = 4
ATT_KROWS = 12
ATT_HPB = 4
CONV_T = 256
CONV_HALO = 16
MOE_TM = 512

VMEM_LIMIT = 56 * 1024 * 1024


def _cparams(sem):
    return pltpu.CompilerParams(dimension_semantics=sem, vmem_limit_bytes=VMEM_LIMIT)


def _silu(x):
    return x * jax.nn.sigmoid(x)


def _rms(x, g):
    return x * lax.rsqrt(jnp.mean(x * x, axis=-1, keepdims=True) + EPS) * g


def _mod_row(mod_ref, grp, col, d):
    return mod_ref[pl.ds(grp, 1), col * d:(col + 1) * d]


def _row_chunks(n_rows, rows, body):
    def step(r, carry):
        body(pl.multiple_of(r * rows, rows))
        return carry
    lax.fori_loop(0, n_rows // rows, step, 0)


def _prenorm_chunk(x, g, shift, scale1):
    return _rms(x, g) * scale1 + shift


def _prenorm_into(h_ref, x_ref, g_ref, mod_ref, grp, shift_col, scale_col, rows=128):
    d = x_ref.shape[-1]
    shift = _mod_row(mod_ref, grp, shift_col, d)
    scale1 = 1.0 + _mod_row(mod_ref, grp, scale_col, d)
    g = g_ref[...]

    def body(r0):
        x = x_ref[pl.ds(r0, rows), :]
        h_ref[pl.ds(r0, rows), :] = _prenorm_chunk(x, g, shift, scale1).astype(h_ref.dtype)

    _row_chunks(x_ref.shape[0], rows, body)


def _mod_kernel(s_ref, w_ref, b_ref, o_ref):
    o_ref[...] = jnp.dot(s_ref[...], w_ref[...], preferred_element_type=F32,
                         precision=lax.Precision.HIGHEST) + b_ref[...]


def _mod_table(s, w_mod, b_mod):
    depth, d, n = w_mod.shape
    tn = 1024
    return pl.pallas_call(
        _mod_kernel,
        out_shape=jax.ShapeDtypeStruct((depth, MOD_ROWS, n), F32),
        grid=(depth, n // tn),
        in_specs=[
            pl.BlockSpec((MOD_ROWS, d), lambda l, j: (0, 0)),
            pl.BlockSpec((None, d, tn), lambda l, j: (l, 0, j)),
            pl.BlockSpec((None, 1, tn), lambda l, j: (l, 0, j)),
        ],
        out_specs=pl.BlockSpec((None, MOD_ROWS, tn), lambda l, j: (l, 0, j)),
        compiler_params=_cparams(("parallel", "parallel")),
        name="mod_table",
    )(s, w_mod, b_mod.reshape(depth, 1, n))


def _in_proj_kernel(x_ref, g_ref, mod_ref, w_ref, o_ref, h_ref, *, tiles_per_batch, n_batch):
    i = pl.program_id(0)

    @pl.when(pl.program_id(1) == 0)
    def _():
        grp = jnp.minimum(i // tiles_per_batch, n_batch)
        _prenorm_into(h_ref, x_ref, g_ref, mod_ref, grp, 0, 1)

    o_ref[...] = jnp.dot(h_ref[...], w_ref[...], preferred_element_type=F32).astype(o_ref.dtype)


def _in_proj(xs, g, mod, w, layer, *, n_batch, seq, tm=1024, tn=1024):
    m, d = xs.shape
    n = w.shape[-1]
    kern = functools.partial(_in_proj_kernel, tiles_per_batch=seq // tm, n_batch=n_batch)
    return pl.pallas_call(
        kern,
        out_shape=jax.ShapeDtypeStruct((m, n), BF16),
        grid=(m // tm, n // tn),
        in_specs=[
            pl.BlockSpec((tm, d), lambda i, j: (i, 0)),
            pl.BlockSpec((None, 1, d), lambda i, j: (layer, 0, 0)),
            pl.BlockSpec((None, MOD_ROWS, N_MOD * d), lambda i, j: (layer, 0, 0)),
            pl.BlockSpec((None, d, tn), lambda i, j: (layer, 0, j)),
        ],
        out_specs=pl.BlockSpec((tm, tn), lambda i, j: (i, j)),
        scratch_shapes=[pltpu.VMEM((tm, d), BF16)],
        compiler_params=_cparams(("parallel", "arbitrary")),
        name="in_proj",
    )(xs, g, mod, w)


def _attn_tables():
    rows = GRID_W
    qi = np.arange(ATT_QROWS)[:, None]
    kr = np.arange(ATT_KROWS)[None, :]
    drs, row_oks = [], []
    for i0, start in ((0, 0), (2 * ATT_QROWS, ATT_QROWS), (rows - ATT_QROWS, rows - ATT_KROWS)):
        i = i0 + qi
        r = start + kr
        r0 = np.clip(i - WIN_H // 2, 0, rows - WIN_H)
        row_oks.append((r >= r0) & (r < r0 + WIN_H))
        drs.append(np.clip(r - i + (WIN_H - 1), 0, 2 * WIN_H - 2))
    qj = np.arange(GRID_W)[:, None]
    kc = np.arange(GRID_W)[None, :]
    ws = np.clip(qj - WIN_W // 2, 0, GRID_W - WIN_W)
    col_ok = (kc >= ws) & (kc < ws + WIN_W)
    dc = np.clip(kc - qj, -(WIN_W - 1), WIN_W - 1) + (WIN_W - 1)
    return np.stack(drs).astype(np.int32), np.stack(row_oks), dc.astype(np.int32), col_ok


def _attn_bias_tiles(rpb):
    depth, heads = rpb.shape[:2]
    dr, row_ok, dc, col_ok = _attn_tables()
    onehot = (dc[:, :, None] == np.arange(2 * WIN_W - 1)[None, None, :]).astype(np.float32)
    t = jnp.einsum("lhdk,jck->lhdjc", rpb.astype(F32), onehot, precision=lax.Precision.HIGHEST)
    t = jnp.where(col_ok, t, -jnp.inf)
    tiles = t[:, :, dr]
    tiles = jnp.where(row_ok[None, None, :, :, :, None, None], tiles, -jnp.inf)
    tiles = tiles.transpose(0, 1, 2, 3, 5, 4, 6)
    return tiles.reshape(depth, heads, 3, ATT_QROWS * GRID_W, ATT_KROWS * GRID_W)


def _attn_kernel(q_ref, k_ref, v_ref, kc_ref, vc_ref, bias_ref, o_ref, *, n_qblocks, max_start,
                 with_ctx_queries):
    rb = pl.program_id(2)
    nk = ATT_KROWS * GRID_W
    scale = HEAD_DIM ** -0.5
    dn = (((1,), (1,)), ((), ()))

    def scores(q, k):
        return lax.dot_general(q, k, dn, preferred_element_type=F32) * scale

    @pl.when(rb < n_qblocks)
    def _():
        start = jnp.clip(rb * ATT_QROWS - WIN_H // 2, 0, max_start)
        tok0 = pl.multiple_of(start * GRID_W, GRID_W)
        for hh in range(ATT_HPB):
            hs = slice(hh * HEAD_DIM, (hh + 1) * HEAD_DIM)
            q = q_ref[:, hs]
            s_loc = scores(q, k_ref[pl.ds(tok0, nk), hs]) + bias_ref[hh]
            s_ctx = scores(q, kc_ref[:, hs])
            m = jnp.maximum(jnp.max(s_loc, axis=-1, keepdims=True), jnp.max(s_ctx, axis=-1, keepdims=True))
            p_loc = jnp.exp(s_loc - m)
            p_ctx = jnp.exp(s_ctx - m)
            l = jnp.sum(p_loc, axis=-1, keepdims=True) + jnp.sum(p_ctx, axis=-1, keepdims=True)
            o = (jnp.dot(p_loc.astype(BF16), v_ref[pl.ds(tok0, nk), hs], preferred_element_type=F32)
                 + jnp.dot(p_ctx.astype(BF16), vc_ref[:, hs], preferred_element_type=F32))
            o_ref[:, hs] = (o / l).astype(o_ref.dtype)

    if with_ctx_queries:
        @pl.when(rb == n_qblocks)
        def _():
            for hh in range(ATT_HPB):
                hs = slice(hh * HEAD_DIM, (hh + 1) * HEAD_DIM)
                s_ctx = scores(q_ref[:, hs], kc_ref[:, hs])
                p_ctx = jnp.exp(s_ctx - jnp.max(s_ctx, axis=-1, keepdims=True))
                l = jnp.sum(p_ctx, axis=-1, keepdims=True)
                o = jnp.dot(p_ctx.astype(BF16), vc_ref[:, hs], preferred_element_type=F32)
                o_ref[:, hs] = (o / l).astype(o_ref.dtype)


def _attention(proj, bias, layer, *, n_batch, seq, ctx_len, with_ctx_queries):
    m = proj.shape[0]
    nq = ATT_QROWS * GRID_W
    assert nq == ctx_len, "context queries reuse the latent query-block shape"
    rows = seq // GRID_W
    n_qblocks = rows // ATT_QROWS
    ctx_blk0 = (n_batch * seq) // ctx_len
    n_hg = N_HEADS // ATT_HPB
    wblk = ATT_HPB * HEAD_DIM

    def qrow(b, rb):
        return jnp.where(rb == n_qblocks, ctx_blk0 + b, b * n_qblocks + rb)

    def pattern(rb):
        return jnp.where(rb == 0, 0, jnp.where(rb >= n_qblocks - 1, 2, 1))

    kern = functools.partial(_attn_kernel, n_qblocks=n_qblocks, max_start=rows - ATT_KROWS,
                             with_ctx_queries=with_ctx_queries)
    return pl.pallas_call(
        kern,
        out_shape=jax.ShapeDtypeStruct((m, N_HEADS * HEAD_DIM), BF16),
        grid=(n_batch, n_hg, n_qblocks + (1 if with_ctx_queries else 0)),
        in_specs=[
            pl.BlockSpec((nq, wblk), lambda b, h, rb: (qrow(b, rb), h)),
            pl.BlockSpec((seq, wblk), lambda b, h, rb: (b, n_hg + h)),
            pl.BlockSpec((seq, wblk), lambda b, h, rb: (b, 2 * n_hg + h)),
            pl.BlockSpec((ctx_len, wblk), lambda b, h, rb: (ctx_blk0 + b, n_hg + h)),
            pl.BlockSpec((ctx_len, wblk), lambda b, h, rb: (ctx_blk0 + b, 2 * n_hg + h)),
            pl.BlockSpec((None, ATT_HPB, None, nq, ATT_KROWS * GRID_W),
                         lambda b, h, rb: (layer, h, pattern(rb), 0, 0)),
        ],
        out_specs=pl.BlockSpec((nq, wblk), lambda b, h, rb: (qrow(b, rb), h)),
        compiler_params=_cparams(("parallel", "parallel", "arbitrary")),
        name="attention",
    )(proj, proj, proj, proj, proj, bias)


def _conv_kernel(a_ref, g_ref, ap_ref, gp_ref, an_ref, gn_ref, w_ref, b_ref, lg_ref, lb_ref,
                 o_ref, hext_ref, y_ref, *, tiles_per_seq, n_latent_tiles):
    i = pl.program_id(0)
    t = CONV_T
    halo = CONV_HALO
    c = a_ref.shape[-1]
    is_ctx = i >= n_latent_tiles
    pos = i % tiles_per_seq
    at_start = jnp.logical_or(is_ctx, pos == 0)
    at_end = jnp.logical_or(is_ctx, pos == tiles_per_seq - 1)

    def glu(a, g):
        return a.astype(F32) * jax.nn.sigmoid(g.astype(F32))

    lanes = 128
    rchunk = 64
    off = halo - CONV_K // 2

    hext_ref[0:halo, :] = jnp.where(at_start, 0.0, glu(ap_ref[...], gp_ref[...]))
    hext_ref[halo + t:halo + t + halo, :] = jnp.where(at_end, 0.0, glu(an_ref[...], gn_ref[...]))

    def glu_body(r0):
        dst = pl.multiple_of(r0 + halo, halo)
        hext_ref[pl.ds(dst, rchunk), :] = glu(a_ref[pl.ds(r0, rchunk), :], g_ref[pl.ds(r0, rchunk), :])

    _row_chunks(t, rchunk, glu_body)

    def chunk_body(ci, carry):
        c0 = pl.multiple_of(ci * lanes, lanes)
        wts = w_ref[:, pl.ds(c0, lanes)]
        bias = b_ref[:, pl.ds(c0, lanes)]
        for r0 in range(0, t, rchunk):
            acc = bias
            for res in range(8):
                n_rows = rchunk + (8 if res else 0)
                part = None
                for base in range(0, CONV_K + off, 8):
                    k = base + res - off
                    if 0 <= k < CONV_K:
                        term = wts[k:k + 1, :] * hext_ref[r0 + base:r0 + base + n_rows, pl.ds(c0, lanes)]
                        part = term if part is None else part + term
                if part is not None:
                    acc = acc + part[res:res + rchunk, :]
            y_ref[r0:r0 + rchunk, pl.ds(c0, lanes)] = acc
        return carry

    lax.fori_loop(0, c // lanes, chunk_body, 0)

    ln_g = lg_ref[...]
    ln_b = lb_ref[...]

    def ln_body(r0):
        y = y_ref[pl.ds(r0, rchunk), :]
        mu = jnp.mean(y, axis=-1, keepdims=True)
        yc = y - mu
        var = jnp.mean(yc * yc, axis=-1, keepdims=True)
        z = yc * lax.rsqrt(var + EPS) * ln_g + ln_b
        o_ref[pl.ds(r0, rchunk), :] = _silu(z).astype(o_ref.dtype)

    _row_chunks(t, rchunk, ln_body)


def _conv_module(proj, w_dw, b_dw, ln_g, ln_b, layer, *, n_batch, seq, ctx_len, a_col, g_col):
    m = proj.shape[0]
    c = w_dw.shape[-1]
    t = CONV_T
    assert ctx_len == t and seq % t == 0
    hb = t // CONV_HALO
    last_hblk = m // CONV_HALO - 1
    n_tiles = m // t
    kern = functools.partial(_conv_kernel, tiles_per_seq=seq // t, n_latent_tiles=(n_batch * seq) // t)

    def prev(i):
        return jnp.maximum(i * hb - 1, 0)

    def nxt(i):
        return jnp.minimum(i * hb + hb, last_hblk)

    vec = lambda: pl.BlockSpec((None, 1, c), lambda i: (layer, 0, 0))
    return pl.pallas_call(
        kern,
        out_shape=jax.ShapeDtypeStruct((m, c), BF16),
        grid=(n_tiles,),
        in_specs=[
            pl.BlockSpec((t, c), lambda i: (i, a_col)),
            pl.BlockSpec((t, c), lambda i: (i, g_col)),
            pl.BlockSpec((CONV_HALO, c), lambda i: (prev(i), a_col)),
            pl.BlockSpec((CONV_HALO, c), lambda i: (prev(i), g_col)),
            pl.BlockSpec((CONV_HALO, c), lambda i: (nxt(i), a_col)),
            pl.BlockSpec((CONV_HALO, c), lambda i: (nxt(i), g_col)),
            pl.BlockSpec((None, CONV_K, c), lambda i: (layer, 0, 0)),
            vec(), vec(), vec(),
        ],
        out_specs=pl.BlockSpec((t, c), lambda i: (i, 0)),
        scratch_shapes=[pltpu.VMEM((t + 2 * CONV_HALO, c), F32), pltpu.VMEM((t, c), F32)],
        compiler_params=_cparams(("parallel",)),
        name="conv_module",
    )(proj, proj, proj, proj, proj, proj, w_dw, b_dw, ln_g, ln_b)


def _merge_kernel(oa_ref, hc_ref, wa_ref, wc_ref, ga_ref, gc_ref, o_ref):
    ya = jnp.dot(oa_ref[...], wa_ref[...], preferred_element_type=F32)
    yc = jnp.dot(hc_ref[...], wc_ref[...], preferred_element_type=F32)
    y = jax.nn.sigmoid(ga_ref[...].astype(F32)) * ya + jax.nn.sigmoid(gc_ref[...].astype(F32)) * yc
    o_ref[...] = y.astype(o_ref.dtype)


def _merge(o_attn, h_conv, wa, wc, proj, layer, *, ga_col0, gc_col0, tm=512, tn=1024):
    m, k = o_attn.shape
    n = wa.shape[-1]
    return pl.pallas_call(
        _merge_kernel,
        out_shape=jax.ShapeDtypeStruct((m, n), BF16),
        grid=(n // tn, m // tm),
        in_specs=[
            pl.BlockSpec((tm, k), lambda j, i: (i, 0)),
            pl.BlockSpec((tm, k), lambda j, i: (i, 0)),
            pl.BlockSpec((None, k, tn), lambda j, i: (layer, 0, j)),
            pl.BlockSpec((None, k, tn), lambda j, i: (layer, 0, j)),
            pl.BlockSpec((tm, tn), lambda j, i: (i, ga_col0 // tn + j)),
            pl.BlockSpec((tm, tn), lambda j, i: (i, gc_col0 // tn + j)),
        ],
        out_specs=pl.BlockSpec((tm, tn), lambda j, i: (i, j)),
        compiler_params=_cparams(("parallel", "parallel")),
        name="merge",
    )(o_attn, h_conv, wa, wc, proj, proj)


def _post_residual(z_chunk, x_ref, g_ref, mod_ref, grp, gate_col, o_ref, rows=128):
    d = x_ref.shape[-1]
    gate = _mod_row(mod_ref, grp, gate_col, d)
    g = g_ref[...]

    def body(r0):
        o_ref[pl.ds(r0, rows), :] = x_ref[pl.ds(r0, rows), :] + gate * _rms(z_chunk(r0), g)

    _row_chunks(x_ref.shape[0], rows, body)


def _matmul_res_kernel(a_ref, w_ref, x_ref, g_ref, mod_ref, o_ref, acc_ref, *,
                       tiles_per_batch, n_batch, gate_col, nk):
    i = pl.program_id(0)
    kk = pl.program_id(1)
    part = jnp.dot(a_ref[...], w_ref[...], preferred_element_type=F32)

    @pl.when(kk == 0)
    def _():
        acc_ref[...] = part

    @pl.when(kk > 0)
    def _():
        acc_ref[...] += part

    @pl.when(kk == nk - 1)
    def _():
        grp = jnp.minimum(i // tiles_per_batch, n_batch)
        rows = 128
        _post_residual(lambda r0: acc_ref[pl.ds(r0, rows), :], x_ref, g_ref, mod_ref, grp, gate_col,
                       o_ref, rows)


def _matmul_residual(a, w, xs, g, mod, layer, *, gate_col, n_batch, seq, m_rows, w_index, tm=512, tk=None):
    k = a.shape[1]
    d = xs.shape[1]
    tk = k if tk is None else tk
    nk = k // tk
    kern = functools.partial(_matmul_res_kernel, tiles_per_batch=seq // tm, n_batch=n_batch,
                             gate_col=gate_col, nk=nk)
    return pl.pallas_call(
        kern,
        out_shape=jax.ShapeDtypeStruct(xs.shape, F32),
        grid=(m_rows // tm, nk),
        in_specs=[
            pl.BlockSpec((tm, tk), lambda i, kk: (i, kk)),
            pl.BlockSpec((None, tk, d), lambda i, kk: (w_index, kk, 0)),
            pl.BlockSpec((tm, d), lambda i, kk: (i, 0)),
            pl.BlockSpec((None, 1, d), lambda i, kk: (layer, 0, 0)),
            pl.BlockSpec((None, MOD_ROWS, N_MOD * d), lambda i, kk: (layer, 0, 0)),
        ],
        out_specs=pl.BlockSpec((tm, d), lambda i, kk: (i, 0)),
        scratch_shapes=[pltpu.VMEM((tm, d), F32)],
        input_output_aliases={2: 0},
        compiler_params=_cparams(("parallel", "arbitrary")),
        name="matmul_residual",
    )(a, w, xs, g, mod)


def _ffn_in_kernel(x_ref, g_ref, mod_ref, wg_ref, wu_ref, o_ref, h_ref, *, tiles_per_batch, n_batch):
    i = pl.program_id(0)

    @pl.when(pl.program_id(1) == 0)
    def _():
        grp = jnp.minimum(i // tiles_per_batch, n_batch)
        _prenorm_into(h_ref, x_ref, g_ref, mod_ref, grp, 3, 4)

    h = h_ref[...]
    a = jnp.dot(h, wg_ref[...], preferred_element_type=F32)
    u = jnp.dot(h, wu_ref[...], preferred_element_type=F32)
    o_ref[...] = (_silu(a) * u).astype(o_ref.dtype)


def _ffn_in(xs, g, mod, wg, wu, layer, j_dense, *, n_batch, seq, m_rows, tm=1024, tn=512):
    d = xs.shape[1]
    f = wg.shape[-1]
    kern = functools.partial(_ffn_in_kernel, tiles_per_batch=seq // tm, n_batch=n_batch)
    return pl.pallas_call(
        kern,
        out_shape=jax.ShapeDtypeStruct((m_rows, f), BF16),
        grid=(m_rows // tm, f // tn),
        in_specs=[
            pl.BlockSpec((tm, d), lambda i, j: (i, 0)),
            pl.BlockSpec((None, 1, d), lambda i, j: (layer, 0, 0)),
            pl.BlockSpec((None, MOD_ROWS, N_MOD * d), lambda i, j: (layer, 0, 0)),
            pl.BlockSpec((None, d, tn), lambda i, j: (j_dense, 0, j)),
            pl.BlockSpec((None, d, tn), lambda i, j: (j_dense, 0, j)),
        ],
        out_specs=pl.BlockSpec((tm, tn), lambda i, j: (i, j)),
        scratch_shapes=[pltpu.VMEM((tm, d), BF16)],
        compiler_params=_cparams(("parallel", "arbitrary")),
        name="ffn_in",
    )(xs, g, mod, wg, wu)


def _router_kernel(x_ref, g_ref, mod_ref, wr_ref, h_ref, ti_ref, tw_ref, *, tiles_per_batch, n_batch):
    i = pl.program_id(0)
    grp = jnp.minimum(i // tiles_per_batch, n_batch)
    d = x_ref.shape[-1]
    shift = _mod_row(mod_ref, grp, 3, d)
    scale1 = 1.0 + _mod_row(mod_ref, grp, 4, d)
    g = g_ref[...]
    rows = 128

    def body(r0):
        h = _prenorm_chunk(x_ref[pl.ds(r0, rows), :], g, shift, scale1)
        h_ref[pl.ds(r0, rows), :] = h.astype(h_ref.dtype)
        logits = jnp.dot(h, wr_ref[...], preferred_element_type=F32, precision=lax.Precision.HIGHEST)
        lane = lax.broadcasted_iota(jnp.int32, logits.shape, 1)
        big = jnp.int32(logits.shape[-1])
        logits = jnp.where(lane < N_EXPERTS, logits, -jnp.inf)
        m1 = jnp.max(logits, axis=-1, keepdims=True)
        i1 = jnp.min(jnp.where(logits == m1, lane, big), axis=-1, keepdims=True)
        rest = jnp.where(lane == i1, -jnp.inf, logits)
        m2 = jnp.max(rest, axis=-1, keepdims=True)
        i2 = jnp.min(jnp.where(rest == m2, lane, big), axis=-1, keepdims=True)
        e2 = jnp.exp(m2 - m1)
        w1 = 1.0 / (1.0 + e2)
        w2 = e2 / (1.0 + e2)
        ti_ref[pl.ds(r0, rows), :] = jnp.where(lane == 0, i1, jnp.where(lane == 1, i2, 0))
        tw_ref[pl.ds(r0, rows), :] = jnp.where(lane == 0, w1, jnp.where(lane == 1, w2, 0.0))

    _row_chunks(x_ref.shape[0], rows, body)


def _router(xs, g, mod, w_router_pad, layer, j_moe, *, n_batch, seq, m_rows, tm=512):
    d = xs.shape[1]
    lanes = w_router_pad.shape[-1]
    kern = functools.partial(_router_kernel, tiles_per_batch=seq // tm, n_batch=n_batch)
    return pl.pallas_call(
        kern,
        out_shape=(jax.ShapeDtypeStruct((m_rows, d), BF16),
                   jax.ShapeDtypeStruct((m_rows, lanes), jnp.int32),
                   jax.ShapeDtypeStruct((m_rows, lanes), F32)),
        grid=(m_rows // tm,),
        in_specs=[
            pl.BlockSpec((tm, d), lambda i: (i, 0)),
            pl.BlockSpec((None, 1, d), lambda i: (layer, 0, 0)),
            pl.BlockSpec((None, MOD_ROWS, N_MOD * d), lambda i: (layer, 0, 0)),
            pl.BlockSpec((None, d, lanes), lambda i: (j_moe, 0, 0)),
        ],
        out_specs=(pl.BlockSpec((tm, d), lambda i: (i, 0)),
                   pl.BlockSpec((tm, lanes), lambda i: (i, 0)),
                   pl.BlockSpec((tm, lanes), lambda i: (i, 0))),
        compiler_params=_cparams(("parallel",)),
        name="router",
    )(xs, g, mod, w_router_pad)


def _expert_block_is_new(te_ref, t):
    return jnp.logical_or(t == 0, te_ref[t] != te_ref[jnp.maximum(t - 1, 0)])


def _moe_up_kernel(te_ref, nu_ref, x_ref, wg_ref, wu_ref, o_ref, wgb_ref, wub_ref):
    t = pl.program_id(1)

    @pl.when(_expert_block_is_new(te_ref, t))
    def _():
        wgb_ref[...] = wg_ref[...].astype(BF16)
        wub_ref[...] = wu_ref[...].astype(BF16)

    @pl.when(t < nu_ref[0])
    def _():
        x = x_ref[...]
        a = jnp.dot(x, wgb_ref[...], preferred_element_type=F32)
        u = jnp.dot(x, wub_ref[...], preferred_element_type=F32)
        o_ref[...] = (_silu(a) * u).astype(o_ref.dtype)

    @pl.when(t >= nu_ref[0])
    def _():
        o_ref[...] = jnp.zeros_like(o_ref)


def _moe_up(tile_expert, n_used, x_sorted, wg, wu, j_moe, *, tn=512):
    r, d = x_sorted.shape
    f = wg.shape[-1]
    tm = MOE_TM
    grid_spec = pltpu.PrefetchScalarGridSpec(
        num_scalar_prefetch=2,
        grid=(f // tn, r // tm),
        in_specs=[
            pl.BlockSpec((tm, d), lambda j, t, te, nu: (t, 0)),
            pl.BlockSpec((None, None, d, tn), lambda j, t, te, nu: (j_moe, te[t], 0, j)),
            pl.BlockSpec((None, None, d, tn), lambda j, t, te, nu: (j_moe, te[t], 0, j)),
        ],
        out_specs=pl.BlockSpec((tm, tn), lambda j, t, te, nu: (t, j)),
        scratch_shapes=[pltpu.VMEM((d, tn), BF16), pltpu.VMEM((d, tn), BF16)],
    )
    return pl.pallas_call(
        _moe_up_kernel,
        out_shape=jax.ShapeDtypeStruct((r, f), BF16),
        grid_spec=grid_spec,
        compiler_params=_cparams(("arbitrary", "arbitrary")),
        name="moe_up",
    )(tile_expert, n_used, x_sorted, wg, wu)


def _moe_down_kernel(te_ref, nu_ref, a_ref, wd_ref, rw_ref, o_ref, wdb_ref):
    t = pl.program_id(1)

    @pl.when(_expert_block_is_new(te_ref, t))
    def _():
        wdb_ref[...] = wd_ref[...].astype(BF16)

    @pl.when(t < nu_ref[0])
    def _():
        y = jnp.dot(a_ref[...], wdb_ref[...], preferred_element_type=F32)
        o_ref[...] = (y * rw_ref[...]).astype(o_ref.dtype)

    @pl.when(t >= nu_ref[0])
    def _():
        o_ref[...] = jnp.zeros_like(o_ref)


def _moe_down(tile_expert, n_used, act, wd, row_w, j_moe, *, tn=512):
    r, f = act.shape
    d = wd.shape[-1]
    tm = MOE_TM
    grid_spec = pltpu.PrefetchScalarGridSpec(
        num_scalar_prefetch=2,
        grid=(d // tn, r // tm),
        in_specs=[
            pl.BlockSpec((tm, f), lambda j, t, te, nu: (t, 0)),
            pl.BlockSpec((None, None, f, tn), lambda j, t, te, nu: (j_moe, te[t], 0, j)),
            pl.BlockSpec((tm, 1), lambda j, t, te, nu: (t, 0)),
        ],
        out_specs=pl.BlockSpec((tm, tn), lambda j, t, te, nu: (t, j)),
        scratch_shapes=[pltpu.VMEM((f, tn), BF16)],
    )
    return pl.pallas_call(
        _moe_down_kernel,
        out_shape=jax.ShapeDtypeStruct((r, d), BF16),
        grid_spec=grid_spec,
        compiler_params=_cparams(("arbitrary", "arbitrary")),
        name="moe_down",
    )(tile_expert, n_used, act, wd, row_w)


def _norm_res_kernel(y0_ref, y1_ref, x_ref, g_ref, mod_ref, o_ref, *, tiles_per_batch, n_batch, gate_col):
    i = pl.program_id(0)
    grp = jnp.minimum(i // tiles_per_batch, n_batch)
    rows = 128
    _post_residual(lambda r0: y0_ref[pl.ds(r0, rows), :].astype(F32) + y1_ref[pl.ds(r0, rows), :].astype(F32),
                   x_ref, g_ref, mod_ref, grp, gate_col, o_ref, rows)


def _norm_residual(y0, y1, xs, g, mod, layer, *, gate_col, n_batch, seq, m_rows, tm=512):
    d = xs.shape[1]
    kern = functools.partial(_norm_res_kernel, tiles_per_batch=seq // tm, n_batch=n_batch, gate_col=gate_col)
    row = lambda: pl.BlockSpec((tm, d), lambda i: (i, 0))
    return pl.pallas_call(
        kern,
        out_shape=jax.ShapeDtypeStruct(xs.shape, F32),
        grid=(m_rows // tm,),
        in_specs=[row(), row(), row(),
                  pl.BlockSpec((None, 1, d), lambda i: (layer, 0, 0)),
                  pl.BlockSpec((None, MOD_ROWS, N_MOD * d), lambda i: (layer, 0, 0))],
        out_specs=row(),
        input_output_aliases={2: 0},
        compiler_params=_cparams(("parallel",)),
        name="norm_residual",
    )(y0, y1, xs, g, mod)


def _route_tables(top_i, top_w, tm):
    n = top_i.shape[0]
    flat_e = top_i.reshape(-1)
    onehot = (flat_e[:, None] == jnp.arange(N_EXPERTS, dtype=jnp.int32)[None, :]).astype(jnp.int32)
    csum = jnp.cumsum(onehot, axis=0)
    rank = jnp.sum((csum - 1) * onehot, axis=1)
    counts = csum[-1]
    padded = ((counts + tm - 1) // tm) * tm
    ends = jnp.cumsum(padded)
    starts = ends - padded
    pos = starts[flat_e] + rank
    r_pad = 2 * n + N_EXPERTS * tm
    n_tiles = r_pad // tm
    row_token = jnp.zeros((r_pad,), jnp.int32).at[pos].set(jnp.arange(2 * n, dtype=jnp.int32) // 2)
    row_w = jnp.zeros((r_pad,), F32).at[pos].set(top_w.reshape(-1))
    n_used = (ends[-1] // tm).astype(jnp.int32)
    tile_start = jnp.arange(n_tiles, dtype=jnp.int32) * tm
    tile_expert = jnp.sum((tile_start[:, None] >= ends[None, :]).astype(jnp.int32), axis=1)
    last = jnp.minimum(tile_expert[jnp.maximum(n_used - 1, 0)], N_EXPERTS - 1)
    tile_expert = jnp.where(jnp.arange(n_tiles) < n_used, jnp.minimum(tile_expert, N_EXPERTS - 1), last)
    return pos.reshape(n, 2), row_token, row_w, tile_expert.astype(jnp.int32), n_used.reshape(1)


def kernel(x, c, ctx, c_ctx, w_mod, b_mod, g_mix_pre, g_mix_post, g_ffn_pre, g_ffn_post, w_in, rpb,
           w_attn_out, conv_dw, conv_db, conv_ln_g, conv_ln_b, w_conv_out, w_out, w_ff_gate, w_ff_up,
           w_ff_down, w_router, w_exp_gate, w_exp_up, w_exp_down):
    n_batch, seq, d = x.shape
    ctx_len = ctx.shape[1]
    depth = w_mod.shape[0]
    n_lat = n_batch * seq
    m_all = n_lat + n_batch * ctx_len
    assert seq == GRID_W * GRID_W and d == N_HEADS * HEAD_DIM and n_batch + 1 <= MOD_ROWS
    geo = dict(n_batch=n_batch, seq=seq)

    w_in_b = w_in.astype(BF16)
    w_ao_b = w_attn_out.astype(BF16)
    w_co_b = w_conv_out.astype(BF16)
    w_o_b = w_out.astype(BF16)
    w_fg_b, w_fu_b, w_fd_b = w_ff_gate.astype(BF16), w_ff_up.astype(BF16), w_ff_down.astype(BF16)
    w_router_pad = jnp.pad(w_router, ((0, 0), (0, 0), (0, 128 - N_EXPERTS)))

    vec = lambda a: a.reshape(depth, 1, -1)
    g_mix_pre, g_mix_post, g_ffn_pre, g_ffn_post = map(vec, (g_mix_pre, g_mix_post, g_ffn_pre, g_ffn_post))
    conv_db, conv_ln_g, conv_ln_b = map(vec, (conv_db, conv_ln_g, conv_ln_b))

    s = jnp.zeros((MOD_ROWS, d), F32).at[:n_batch].set(_silu(c)).at[n_batch].set(_silu(c_ctx))
    mod = _mod_table(s, w_mod, b_mod)
    bias = _attn_bias_tiles(rpb)

    xs = jnp.concatenate([x.reshape(n_lat, d), ctx.reshape(n_batch * ctx_len, d)], axis=0)
    attn_w = N_HEADS * HEAD_DIM
    conv_w = conv_dw.shape[-1]
    u_col0 = 3 * attn_w
    ga_col0 = u_col0 + 2 * conv_w
    gc_col0 = ga_col0 + d

    for layer in range(depth):
        last = layer == depth - 1
        m_rows = n_lat if last else m_all
        j = layer // 2

        proj = _in_proj(xs, g_mix_pre, mod, w_in_b, layer, **geo)
        o_attn = _attention(proj, bias, layer, ctx_len=ctx_len, with_ctx_queries=not last, **geo)
        h_conv = _conv_module(proj, conv_dw, conv_db, conv_ln_g, conv_ln_b, layer, ctx_len=ctx_len,
                              a_col=u_col0 // conv_w, g_col=u_col0 // conv_w + 1, **geo)
        y = _merge(o_attn, h_conv, w_ao_b, w_co_b, proj, layer, ga_col0=ga_col0, gc_col0=gc_col0)
        xs = _matmul_residual(y, w_o_b, xs, g_mix_post, mod, layer, gate_col=2, m_rows=m_rows,
                              w_index=layer, **geo)

        if layer % 2 == 0:
            act = _ffn_in(xs, g_ffn_pre, mod, w_fg_b, w_fu_b, layer, j, m_rows=m_rows, **geo)
            xs = _matmul_residual(act, w_fd_b, xs, g_ffn_post, mod, layer, gate_col=5, m_rows=m_rows,
                                  w_index=j, tk=2816, **geo)
        else:
            h, top_i, top_w = _router(xs, g_ffn_pre, mod, w_router_pad, layer, j, m_rows=m_rows, **geo)
            pos, row_token, row_w, tile_expert, n_used = _route_tables(top_i[:, :2], top_w[:, :2], MOE_TM)
            x_sorted = jnp.take(h, row_token, axis=0)
            act = _moe_up(tile_expert, n_used, x_sorted, w_exp_gate, w_exp_up, j)
            y_sorted = _moe_down(tile_expert, n_used, act, w_exp_down, row_w[:, None], j)
            y0 = jnp.take(y_sorted, pos[:, 0], axis=0)
            y1 = jnp.take(y_sorted, pos[:, 1], axis=0)
            xs = _norm_residual(y0, y1, xs, g_ffn_post, mod, layer, gate_col=5, m_rows=m_rows, **geo)

    return xs[:n_lat].reshape(n_batch, seq, d)
```

```python
import functools

import numpy as np
import jax
import jax.numpy as jnp
from jax import lax
from jax.experimental import pallas as pl
from jax.experimental.pallas import tpu as pltpu

F32 = jnp.float32
BF16 = jnp.bfloat16

EPS = 1e-6
GRID_W = 64
N_HEADS = 16
HEAD_DIM = 128
CONV_K = 31
WIN_H = 8
WIN_W = 16
N_MOD = 6
N_EXPERTS = 8
MOD_ROWS = 8

ATT_QROWS = 4
ATT_KROWS = 12
ATT_HPB = 4
CONV_T = 256
CONV_HALO = 16
MOE_TM = 1024

VMEM_LIMIT = 56 * 1024 * 1024


def _cparams(sem):
    return pltpu.CompilerParams(dimension_semantics=sem, vmem_limit_bytes=VMEM_LIMIT)


def _silu(x):
    return x * jax.nn.sigmoid(x)


def _rms(x, g):
    return x * lax.rsqrt(jnp.mean(x * x, axis=-1, keepdims=True) + EPS) * g


def _mod_row(mod_ref, grp, col, d):
    return mod_ref[pl.ds(grp, 1), col * d:(col + 1) * d]


def _row_chunks(n_rows, rows, body):
    def step(r, carry):
        body(pl.multiple_of(r * rows, rows))
        return carry
    lax.fori_loop(0, n_rows // rows, step, 0)


def _prenorm_chunk(x, g, shift, scale1):
    return _rms(x, g) * scale1 + shift


def _prenorm_into(h_ref, x_ref, g_ref, mod_ref, grp, shift_col, scale_col, rows=128):
    d = x_ref.shape[-1]
    shift = _mod_row(mod_ref, grp, shift_col, d)
    scale1 = 1.0 + _mod_row(mod_ref, grp, scale_col, d)
    g = g_ref[...]

    def body(r0):
        x = x_ref[pl.ds(r0, rows), :]
        h_ref[pl.ds(r0, rows), :] = _prenorm_chunk(x, g, shift, scale1).astype(h_ref.dtype)

    _row_chunks(x_ref.shape[0], rows, body)


def _mod_kernel(s_ref, w_ref, b_ref, o_ref):
    o_ref[...] = jnp.dot(s_ref[...], w_ref[...], preferred_element_type=F32,
                         precision=lax.Precision.HIGHEST) + b_ref[...]


def _mod_table(s, w_mod, b_mod):
    depth, d, n = w_mod.shape
    tn = 1024
    return pl.pallas_call(
        _mod_kernel,
        out_shape=jax.ShapeDtypeStruct((depth, MOD_ROWS, n), F32),
        grid=(depth, n // tn),
        in_specs=[
            pl.BlockSpec((MOD_ROWS, d), lambda l, j: (0, 0)),
            pl.BlockSpec((None, d, tn), lambda l, j: (l, 0, j)),
            pl.BlockSpec((None, 1, tn), lambda l, j: (l, 0, j)),
        ],
        out_specs=pl.BlockSpec((None, MOD_ROWS, tn), lambda l, j: (l, 0, j)),
        compiler_params=_cparams(("parallel", "parallel")),
        name="mod_table",
    )(s, w_mod, b_mod.reshape(depth, 1, n))


def _in_proj_kernel(x_ref, g_ref, mod_ref, w_ref, o_ref, h_ref, *, tiles_per_batch, n_batch):
    i = pl.program_id(0)

    @pl.when(pl.program_id(1) == 0)
    def _():
        grp = jnp.minimum(i // tiles_per_batch, n_batch)
        _prenorm_into(h_ref, x_ref, g_ref, mod_ref, grp, 0, 1)

    o_ref[...] = jnp.dot(h_ref[...], w_ref[...], preferred_element_type=F32).astype(o_ref.dtype)


def _in_proj(xs, g, mod, w, layer, *, n_batch, seq, tm=1024, tn=1024):
    m, d = xs.shape
    n = w.shape[-1]
    kern = functools.partial(_in_proj_kernel, tiles_per_batch=seq // tm, n_batch=n_batch)
    return pl.pallas_call(
        kern,
        out_shape=jax.ShapeDtypeStruct((m, n), BF16),
        grid=(m // tm, n // tn),
        in_specs=[
            pl.BlockSpec((tm, d), lambda i, j: (i, 0)),
            pl.BlockSpec((None, 1, d), lambda i, j: (layer, 0, 0)),
            pl.BlockSpec((None, MOD_ROWS, N_MOD * d), lambda i, j: (layer, 0, 0)),
            pl.BlockSpec((None, d, tn), lambda i, j: (layer, 0, j)),
        ],
        out_specs=pl.BlockSpec((tm, tn), lambda i, j: (i, j)),
        scratch_shapes=[pltpu.VMEM((tm, d), BF16)],
        compiler_params=_cparams(("parallel", "arbitrary")),
        name="in_proj",
    )(xs, g, mod, w)


def _attn_tables():
    rows = GRID_W
    qi = np.arange(ATT_QROWS)[:, None]
    kr = np.arange(ATT_KROWS)[None, :]
    drs, row_oks = [], []
    for i0, start in ((0, 0), (2 * ATT_QROWS, ATT_QROWS), (rows - ATT_QROWS, rows - ATT_KROWS)):
        i = i0 + qi
        r = start + kr
        r0 = np.clip(i - WIN_H // 2, 0, rows - WIN_H)
        row_oks.append((r >= r0) & (r < r0 + WIN_H))
        drs.append(np.clip(r - i + (WIN_H - 1), 0, 2 * WIN_H - 2))
    qj = np.arange(GRID_W)[:, None]
    kc = np.arange(GRID_W)[None, :]
    ws = np.clip(qj - WIN_W // 2, 0, GRID_W - WIN_W)
    col_ok = (kc >= ws) & (kc < ws + WIN_W)
    dc = np.clip(kc - qj, -(WIN_W - 1), WIN_W - 1) + (WIN_W - 1)
    return np.stack(drs).astype(np.int32), np.stack(row_oks), dc.astype(np.int32), col_ok


def _attn_pair_plan():
    dr, row_ok, _, _ = _attn_tables()
    entries, plan = [], []
    for pat in range(dr.shape[0]):
        per_q = []
        for qi in range(ATT_QROWS):
            per_p = []
            for p in range(ATT_KROWS // 2):
                halves = tuple(int(dr[pat, qi, kr]) if row_ok[pat, qi, kr] else None for kr in (2 * p, 2 * p + 1))
                if halves == (None, None):
                    per_p.append(None)
                    continue
                if halves not in entries:
                    entries.append(halves)
                per_p.append(entries.index(halves))
            per_q.append(per_p)
        plan.append(per_q)
    return entries, plan


_LOG2E = float(np.log2(np.e))


def _attn_bias_slabs(rpb):
    _, _, dc, col_ok = _attn_tables()
    entries, _ = _attn_pair_plan()
    onehot = (dc[:, :, None] == np.arange(2 * WIN_W - 1)[None, None, :]).astype(np.float32)
    t = jnp.einsum("lhdk,jck->lhdjc", rpb.astype(F32), onehot, precision=lax.Precision.HIGHEST)
    t = jnp.where(col_ok, t * _LOG2E, -jnp.inf)
    masked = jnp.full(t.shape[:2] + t.shape[3:], -jnp.inf, F32)
    half = lambda d: masked if d is None else t[:, :, d]
    return jnp.stack([jnp.concatenate([half(a), half(b)], axis=-1) for a, b in entries], axis=2)


def _attn_kernel(q_ref, k_ref, v_ref, kc_ref, vc_ref, slab_ref, o_ref, bias_ref, *,
                 n_qblocks, max_start, with_ctx_queries):
    rb = pl.program_id(2)
    nk = ATT_KROWS * GRID_W
    scale2 = HEAD_DIM ** -0.5 * _LOG2E
    dn = (((1,), (1,)), ((), ()))
    _, plan = _attn_pair_plan()

    def scores(q, k):
        return lax.dot_general(q, k, dn, preferred_element_type=F32) * scale2

    def assemble(pattern):
        masked = jnp.full((GRID_W, 2 * GRID_W), -jnp.inf, F32)
        for hh in range(ATT_HPB):
            for qi in range(ATT_QROWS):
                for p, e in enumerate(plan[pattern][qi]):
                    blk = masked if e is None else slab_ref[hh, e]
                    bias_ref[hh, qi * GRID_W:(qi + 1) * GRID_W, 2 * p * GRID_W:2 * (p + 1) * GRID_W] = blk

    for pattern, first_rb in ((0, 0), (1, 1), (2, n_qblocks - 1)):
        pl.when(rb == first_rb)(functools.partial(assemble, pattern))

    @pl.when(rb < n_qblocks)
    def _():
        start = jnp.clip(rb * ATT_QROWS - WIN_H // 2, 0, max_start)
        tok0 = pl.multiple_of(start * GRID_W, GRID_W)
        heads = [slice(hh * HEAD_DIM, (hh + 1) * HEAD_DIM) for hh in range(ATT_HPB)]

        s_all = [(scores(q_ref[:, hs], k_ref[pl.ds(tok0, nk), hs]) + bias_ref[hh], scores(q_ref[:, hs], kc_ref[:, hs]))
                 for hh, hs in enumerate(heads)]
        for hs, (s_loc, s_ctx) in zip(heads, s_all):
            m = jnp.maximum(jnp.max(s_loc, axis=-1, keepdims=True), jnp.max(s_ctx, axis=-1, keepdims=True))
            p_loc = jnp.exp2(s_loc - m)
            p_ctx = jnp.exp2(s_ctx - m)
            l = jnp.sum(p_loc, axis=-1, keepdims=True) + jnp.sum(p_ctx, axis=-1, keepdims=True)
            o = (jnp.dot(p_loc.astype(BF16), v_ref[pl.ds(tok0, nk), hs], preferred_element_type=F32)
                 + jnp.dot(p_ctx.astype(BF16), vc_ref[:, hs], preferred_element_type=F32))
            o_ref[:, hs] = (o / l).astype(o_ref.dtype)

    if with_ctx_queries:
        @pl.when(rb == n_qblocks)
        def _():
            for hh in range(ATT_HPB):
                hs = slice(hh * HEAD_DIM, (hh + 1) * HEAD_DIM)
                s_ctx = scores(q_ref[:, hs], kc_ref[:, hs])
                p_ctx = jnp.exp2(s_ctx - jnp.max(s_ctx, axis=-1, keepdims=True))
                l = jnp.sum(p_ctx, axis=-1, keepdims=True)
                o = jnp.dot(p_ctx.astype(BF16), vc_ref[:, hs], preferred_element_type=F32)
                o_ref[:, hs] = (o / l).astype(o_ref.dtype)


def _attention(proj, slabs, layer, *, n_batch, seq, ctx_len, with_ctx_queries):
    m = proj.shape[0]
    nq = ATT_QROWS * GRID_W
    assert nq == ctx_len, "context queries reuse the latent query-block shape"
    rows = seq // GRID_W
    n_qblocks = rows // ATT_QROWS
    ctx_blk0 = (n_batch * seq) // ctx_len
    n_hg = N_HEADS // ATT_HPB
    wblk = ATT_HPB * HEAD_DIM

    def qrow(b, rb):
        return jnp.where(rb == n_qblocks, ctx_blk0 + b, b * n_qblocks + rb)

    assert n_qblocks >= 3, "first / interior / last bias patterns need distinct query blocks"
    n_entries = slabs.shape[2]
    kern = functools.partial(_attn_kernel, n_qblocks=n_qblocks, max_start=rows - ATT_KROWS,
                             with_ctx_queries=with_ctx_queries)
    return pl.pallas_call(
        kern,
        out_shape=jax.ShapeDtypeStruct((m, N_HEADS * HEAD_DIM), BF16),
        grid=(n_batch, n_hg, n_qblocks + (1 if with_ctx_queries else 0)),
        in_specs=[
            pl.BlockSpec((nq, wblk), lambda b, h, rb: (qrow(b, rb), h)),
            pl.BlockSpec((seq, wblk), lambda b, h, rb: (b, n_hg + h)),
            pl.BlockSpec((seq, wblk), lambda b, h, rb: (b, 2 * n_hg + h)),
            pl.BlockSpec((ctx_len, wblk), lambda b, h, rb: (ctx_blk0 + b, n_hg + h)),
            pl.BlockSpec((ctx_len, wblk), lambda b, h, rb: (ctx_blk0 + b, 2 * n_hg + h)),
            pl.BlockSpec((None, ATT_HPB, n_entries, GRID_W, 2 * GRID_W), lambda b, h, rb: (layer, h, 0, 0, 0)),
        ],
        out_specs=pl.BlockSpec((nq, wblk), lambda b, h, rb: (qrow(b, rb), h)),
        scratch_shapes=[pltpu.VMEM((ATT_HPB, nq, ATT_KROWS * GRID_W), F32)],
        compiler_params=_cparams(("parallel", "parallel", "arbitrary")),
        name="attention",
    )(proj, proj, proj, proj, proj, slabs)


def _conv_kernel(a_ref, g_ref, ap_ref, gp_ref, an_ref, gn_ref, w_ref, b_ref, lg_ref, lb_ref,
                 o_ref, hext_ref, y_ref, *, tiles_per_seq, n_latent_tiles):
    i = pl.program_id(0)
    t = CONV_T
    halo = CONV_HALO
    c = a_ref.shape[-1]
    is_ctx = i >= n_latent_tiles
    pos = i % tiles_per_seq
    at_start = jnp.logical_or(is_ctx, pos == 0)
    at_end = jnp.logical_or(is_ctx, pos == tiles_per_seq - 1)

    def glu(a, g):
        return a.astype(F32) * jax.nn.sigmoid(g.astype(F32))

    lanes = 128
    rchunk = 64
    off = halo - CONV_K // 2

    hext_ref[0:halo, :] = jnp.where(at_start, 0.0, glu(ap_ref[...], gp_ref[...]))
    hext_ref[halo + t:halo + t + halo, :] = jnp.where(at_end, 0.0, glu(an_ref[...], gn_ref[...]))

    def glu_body(r0):
        dst = pl.multiple_of(r0 + halo, halo)
        hext_ref[pl.ds(dst, rchunk), :] = glu(a_ref[pl.ds(r0, rchunk), :], g_ref[pl.ds(r0, rchunk), :])

    _row_chunks(t, rchunk, glu_body)

    def chunk_body(ci, carry):
        c0 = pl.multiple_of(ci * lanes, lanes)
        wts = w_ref[:, pl.ds(c0, lanes)]
        bias = b_ref[:, pl.ds(c0, lanes)]
        for r0 in range(0, t, rchunk):
            acc = bias
            for res in range(8):
                n_rows = rchunk + (8 if res else 0)
                part = None
                for base in range(0, CONV_K + off, 8):
                    k = base + res - off
                    if 0 <= k < CONV_K:
                        term = wts[k:k + 1, :] * hext_ref[r0 + base:r0 + base + n_rows, pl.ds(c0, lanes)]
                        part = term if part is None else part + term
                if part is not None:
                    acc = acc + part[res:res + rchunk, :]
            y_ref[r0:r0 + rchunk, pl.ds(c0, lanes)] = acc
        return carry

    lax.fori_loop(0, c // lanes, chunk_body, 0)

    ln_g = lg_ref[...]
    ln_b = lb_ref[...]

    def ln_body(r0):
        y = y_ref[pl.ds(r0, rchunk), :]
        mu = jnp.mean(y, axis=-1, keepdims=True)
        yc = y - mu
        var = jnp.mean(yc * yc, axis=-1, keepdims=True)
        z = yc * lax.rsqrt(var + EPS) * ln_g + ln_b
        o_ref[pl.ds(r0, rchunk), :] = _silu(z).astype(o_ref.dtype)

    _row_chunks(t, rchunk, ln_body)


def _conv_module(proj, w_dw, b_dw, ln_g, ln_b, layer, *, n_batch, seq, ctx_len, a_col, g_col):
    m = proj.shape[0]
    c = w_dw.shape[-1]
    t = CONV_T
    assert ctx_len == t and seq % t == 0
    hb = t // CONV_HALO
    last_hblk = m // CONV_HALO - 1
    n_tiles = m // t
    kern = functools.partial(_conv_kernel, tiles_per_seq=seq // t, n_latent_tiles=(n_batch * seq) // t)

    def prev(i):
        return jnp.maximum(i * hb - 1, 0)

    def nxt(i):
        return jnp.minimum(i * hb + hb, last_hblk)

    vec = lambda: pl.BlockSpec((None, 1, c), lambda i: (layer, 0, 0))
    return pl.pallas_call(
        kern,
        out_shape=jax.ShapeDtypeStruct((m, c), BF16),
        grid=(n_tiles,),
        in_specs=[
            pl.BlockSpec((t, c), lambda i: (i, a_col)),
            pl.BlockSpec((t, c), lambda i: (i, g_col)),
            pl.BlockSpec((CONV_HALO, c), lambda i: (prev(i), a_col)),
            pl.BlockSpec((CONV_HALO, c), lambda i: (prev(i), g_col)),
            pl.BlockSpec((CONV_HALO, c), lambda i: (nxt(i), a_col)),
            pl.BlockSpec((CONV_HALO, c), lambda i: (nxt(i), g_col)),
            pl.BlockSpec((None, CONV_K, c), lambda i: (layer, 0, 0)),
            vec(), vec(), vec(),
        ],
        out_specs=pl.BlockSpec((t, c), lambda i: (i, 0)),
        scratch_shapes=[pltpu.VMEM((t + 2 * CONV_HALO, c), F32), pltpu.VMEM((t, c), F32)],
        compiler_params=_cparams(("parallel",)),
        name="conv_module",
    )(proj, proj, proj, proj, proj, proj, w_dw, b_dw, ln_g, ln_b)


def _merge_kernel(oa_ref, hc_ref, wa_ref, wc_ref, ga_ref, gc_ref, o_ref):
    ya = jnp.dot(oa_ref[...], wa_ref[...], preferred_element_type=F32)
    yc = jnp.dot(hc_ref[...], wc_ref[...], preferred_element_type=F32)
    y = jax.nn.sigmoid(ga_ref[...].astype(F32)) * ya + jax.nn.sigmoid(gc_ref[...].astype(F32)) * yc
    o_ref[...] = y.astype(o_ref.dtype)


def _merge(o_attn, h_conv, wa, wc, proj, layer, *, ga_col0, gc_col0, tm=512, tn=1024):
    m, k = o_attn.shape
    n = wa.shape[-1]
    return pl.pallas_call(
        _merge_kernel,
        out_shape=jax.ShapeDtypeStruct((m, n), BF16),
        grid=(n // tn, m // tm),
        in_specs=[
            pl.BlockSpec((tm, k), lambda j, i: (i, 0)),
            pl.BlockSpec((tm, k), lambda j, i: (i, 0)),
            pl.BlockSpec((None, k, tn), lambda j, i: (layer, 0, j)),
            pl.BlockSpec((None, k, tn), lambda j, i: (layer, 0, j)),
            pl.BlockSpec((tm, tn), lambda j, i: (i, ga_col0 // tn + j)),
            pl.BlockSpec((tm, tn), lambda j, i: (i, gc_col0 // tn + j)),
        ],
        out_specs=pl.BlockSpec((tm, tn), lambda j, i: (i, j)),
        compiler_params=_cparams(("parallel", "parallel")),
        name="merge",
    )(o_attn, h_conv, wa, wc, proj, proj)


def _post_residual(z_chunk, x_ref, g_ref, mod_ref, grp, gate_col, o_ref, rows=128):
    d = x_ref.shape[-1]
    gate = _mod_row(mod_ref, grp, gate_col, d)
    g = g_ref[...]

    def body(r0):
        o_ref[pl.ds(r0, rows), :] = x_ref[pl.ds(r0, rows), :] + gate * _rms(z_chunk(r0), g)

    _row_chunks(x_ref.shape[0], rows, body)


def _matmul_res_kernel(a_ref, w_ref, x_ref, g_ref, mod_ref, o_ref, acc_ref, *,
                       tiles_per_batch, n_batch, gate_col, nk):
    i = pl.program_id(0)
    kk = pl.program_id(1)
    part = jnp.dot(a_ref[...], w_ref[...], preferred_element_type=F32)

    @pl.when(kk == 0)
    def _():
        acc_ref[...] = part

    @pl.when(kk > 0)
    def _():
        acc_ref[...] += part

    @pl.when(kk == nk - 1)
    def _():
        grp = jnp.minimum(i // tiles_per_batch, n_batch)
        rows = 128
        _post_residual(lambda r0: acc_ref[pl.ds(r0, rows), :], x_ref, g_ref, mod_ref, grp, gate_col,
                       o_ref, rows)


def _matmul_residual(a, w, xs, g, mod, layer, *, gate_col, n_batch, seq, m_rows, w_index, tm=512, tk=None):
    k = a.shape[1]
    d = xs.shape[1]
    tk = k if tk is None else tk
    nk = k // tk
    kern = functools.partial(_matmul_res_kernel, tiles_per_batch=seq // tm, n_batch=n_batch,
                             gate_col=gate_col, nk=nk)
    return pl.pallas_call(
        kern,
        out_shape=jax.ShapeDtypeStruct(xs.shape, F32),
        grid=(m_rows // tm, nk),
        in_specs=[
            pl.BlockSpec((tm, tk), lambda i, kk: (i, kk)),
            pl.BlockSpec((None, tk, d), lambda i, kk: (w_index, kk, 0)),
            pl.BlockSpec((tm, d), lambda i, kk: (i, 0)),
            pl.BlockSpec((None, 1, d), lambda i, kk: (layer, 0, 0)),
            pl.BlockSpec((None, MOD_ROWS, N_MOD * d), lambda i, kk: (layer, 0, 0)),
        ],
        out_specs=pl.BlockSpec((tm, d), lambda i, kk: (i, 0)),
        scratch_shapes=[pltpu.VMEM((tm, d), F32)],
        input_output_aliases={2: 0},
        compiler_params=_cparams(("parallel", "arbitrary")),
        name="matmul_residual",
    )(a, w, xs, g, mod)


def _ffn_in_kernel(x_ref, g_ref, mod_ref, wg_ref, wu_ref, o_ref, h_ref, *, tiles_per_batch, n_batch):
    i = pl.program_id(0)

    @pl.when(pl.program_id(1) == 0)
    def _():
        grp = jnp.minimum(i // tiles_per_batch, n_batch)
        _prenorm_into(h_ref, x_ref, g_ref, mod_ref, grp, 3, 4)

    h = h_ref[...]
    a = jnp.dot(h, wg_ref[...], preferred_element_type=F32)
    u = jnp.dot(h, wu_ref[...], preferred_element_type=F32)
    o_ref[...] = (_silu(a) * u).astype(o_ref.dtype)


def _ffn_in(xs, g, mod, wg, wu, layer, j_dense, *, n_batch, seq, m_rows, tm=1024, tn=512):
    d = xs.shape[1]
    f = wg.shape[-1]
    kern = functools.partial(_ffn_in_kernel, tiles_per_batch=seq // tm, n_batch=n_batch)
    return pl.pallas_call(
        kern,
        out_shape=jax.ShapeDtypeStruct((m_rows, f), BF16),
        grid=(m_rows // tm, f // tn),
        in_specs=[
            pl.BlockSpec((tm, d), lambda i, j: (i, 0)),
            pl.BlockSpec((None, 1, d), lambda i, j: (layer, 0, 0)),
            pl.BlockSpec((None, MOD_ROWS, N_MOD * d), lambda i, j: (layer, 0, 0)),
            pl.BlockSpec((None, d, tn), lambda i, j: (j_dense, 0, j)),
            pl.BlockSpec((None, d, tn), lambda i, j: (j_dense, 0, j)),
        ],
        out_specs=pl.BlockSpec((tm, tn), lambda i, j: (i, j)),
        scratch_shapes=[pltpu.VMEM((tm, d), BF16)],
        compiler_params=_cparams(("parallel", "arbitrary")),
        name="ffn_in",
    )(xs, g, mod, wg, wu)


def _split_bf16(a):
    hi = a.astype(BF16)
    return hi, (a - hi.astype(F32)).astype(BF16)


def _router_kernel(x_ref, g_ref, mod_ref, wh_ref, wl_ref, h_ref, ti_ref, tw_ref, *, tiles_per_batch, n_batch):
    i = pl.program_id(0)
    grp = jnp.minimum(i // tiles_per_batch, n_batch)
    d = x_ref.shape[-1]
    shift = _mod_row(mod_ref, grp, 3, d)
    scale1 = 1.0 + _mod_row(mod_ref, grp, 4, d)
    g = g_ref[...]
    rows = 256

    def body(r0):
        h = _prenorm_chunk(x_ref[pl.ds(r0, rows), :], g, shift, scale1)
        h_ref[pl.ds(r0, rows), :] = h.astype(h_ref.dtype)
        h_hi, h_lo = _split_bf16(h)
        dot = functools.partial(jnp.dot, preferred_element_type=F32)
        logits = dot(h_hi, wh_ref[...]) + (dot(h_hi, wl_ref[...]) + dot(h_lo, wh_ref[...]))
        lane =lax.broadcasted_iota(jnp.int32, logits.shape, 1)
        big = jnp.int32(logits.shape[-1])
        logits = jnp.where(lane < N_EXPERTS, logits, -jnp.inf)
        m1 = jnp.max(logits, axis=-1, keepdims=True)
        i1 = jnp.min(jnp.where(logits == m1, lane, big), axis=-1, keepdims=True)
        rest = jnp.where(lane == i1, -jnp.inf, logits)
        m2 = jnp.max(rest, axis=-1, keepdims=True)
        i2 = jnp.min(jnp.where(rest == m2, lane, big), axis=-1, keepdims=True)
        e2 = jnp.exp(m2 - m1)
        w1 = 1.0 / (1.0 + e2)
        w2 = e2 / (1.0 + e2)
        ti_ref[pl.ds(r0, rows), :] = jnp.where(lane == 0, i1, jnp.where(lane == 1, i2, 0))
        tw_ref[pl.ds(r0, rows), :] = jnp.where(lane == 0, w1, jnp.where(lane == 1, w2, 0.0))

    _row_chunks(x_ref.shape[0], rows, body)


def _router(xs, g, mod, w_hi, w_lo, layer, j_moe, *, n_batch, seq, m_rows, tm=512):
    d = xs.shape[1]
    lanes = w_hi.shape[-1]
    kern = functools.partial(_router_kernel, tiles_per_batch=seq // tm, n_batch=n_batch)
    return pl.pallas_call(
        kern,
        out_shape=(jax.ShapeDtypeStruct((m_rows, d), BF16),
                   jax.ShapeDtypeStruct((m_rows, lanes), jnp.int32),
                   jax.ShapeDtypeStruct((m_rows, lanes), F32)),
        grid=(m_rows // tm,),
        in_specs=[
            pl.BlockSpec((tm, d), lambda i: (i, 0)),
            pl.BlockSpec((None, 1, d), lambda i: (layer, 0, 0)),
            pl.BlockSpec((None, MOD_ROWS, N_MOD * d), lambda i: (layer, 0, 0)),
            pl.BlockSpec((None, d, lanes), lambda i: (j_moe, 0, 0)),
            pl.BlockSpec((None, d, lanes), lambda i: (j_moe, 0, 0)),
        ],
        out_specs=(pl.BlockSpec((tm, d), lambda i: (i, 0)),
                   pl.BlockSpec((tm, lanes), lambda i: (i, 0)),
                   pl.BlockSpec((tm, lanes), lambda i: (i, 0))),
        compiler_params=_cparams(("parallel",)),
        name="router",
    )(xs, g, mod, w_hi, w_lo)


def _expert_block_is_new(te_ref, t):
    return jnp.logical_or(t == 0, te_ref[t] != te_ref[jnp.maximum(t - 1, 0)])


def _moe_up_kernel(te_ref, nu_ref, x_ref, wg_ref, wu_ref, o_ref, wgb_ref, wub_ref):
    t = pl.program_id(1)

    @pl.when(_expert_block_is_new(te_ref, t))
    def _():
        wgb_ref[...] = wg_ref[...].astype(BF16)
        wub_ref[...] = wu_ref[...].astype(BF16)

    @pl.when(t < nu_ref[0])
    def _():
        x = x_ref[...]
        a = jnp.dot(x, wgb_ref[...], preferred_element_type=F32)
        u = jnp.dot(x, wub_ref[...], preferred_element_type=F32)
        o_ref[...] = (_silu(a) * u).astype(o_ref.dtype)

    @pl.when(t >= nu_ref[0])
    def _():
        o_ref[...] = jnp.zeros_like(o_ref)


def _moe_up(tile_expert, n_used, x_sorted, wg, wu, j_moe, *, tn=512):
    r, d = x_sorted.shape
    f = wg.shape[-1]
    tm = MOE_TM
    grid_spec = pltpu.PrefetchScalarGridSpec(
        num_scalar_prefetch=2,
        grid=(f // tn, r // tm),
        in_specs=[
            pl.BlockSpec((tm, d), lambda j, t, te, nu: (t, 0)),
            pl.BlockSpec((None, None, d, tn), lambda j, t, te, nu: (j_moe, te[t], 0, j)),
            pl.BlockSpec((None, None, d, tn), lambda j, t, te, nu: (j_moe, te[t], 0, j)),
        ],
        out_specs=pl.BlockSpec((tm, tn), lambda j, t, te, nu: (t, j)),
        scratch_shapes=[pltpu.VMEM((d, tn), BF16), pltpu.VMEM((d, tn), BF16)],
    )
    return pl.pallas_call(
        _moe_up_kernel,
        out_shape=jax.ShapeDtypeStruct((r, f), BF16),
        grid_spec=grid_spec,
        compiler_params=_cparams(("arbitrary", "arbitrary")),
        name="moe_up",
    )(tile_expert, n_used, x_sorted, wg, wu)


def _moe_down_kernel(te_ref, nu_ref, a_ref, wd_ref, rw_ref, o_ref, wdb_ref):
    t = pl.program_id(1)

    @pl.when(_expert_block_is_new(te_ref, t))
    def _():
        wdb_ref[...] = wd_ref[...].astype(BF16)

    @pl.when(t < nu_ref[0])
    def _():
        y = jnp.dot(a_ref[...], wdb_ref[...], preferred_element_type=F32)
        o_ref[...] = (y * rw_ref[...]).astype(o_ref.dtype)

    @pl.when(t >= nu_ref[0])
    def _():
        o_ref[...] = jnp.zeros_like(o_ref)


def _moe_down(tile_expert, n_used, act, wd, row_w, j_moe, *, tn=512):
    r, f = act.shape
    d = wd.shape[-1]
    tm = MOE_TM
    grid_spec = pltpu.PrefetchScalarGridSpec(
        num_scalar_prefetch=2,
        grid=(d // tn, r // tm),
        in_specs=[
            pl.BlockSpec((tm, f), lambda j, t, te, nu: (t, 0)),
            pl.BlockSpec((None, None, f, tn), lambda j, t, te, nu: (j_moe, te[t], 0, j)),
            pl.BlockSpec((tm, 1), lambda j, t, te, nu: (t, 0)),
        ],
        out_specs=pl.BlockSpec((tm, tn), lambda j, t, te, nu: (t, j)),
        scratch_shapes=[pltpu.VMEM((f, tn), BF16)],
    )
    return pl.pallas_call(
        _moe_down_kernel,
        out_shape=jax.ShapeDtypeStruct((r, d), BF16),
        grid_spec=grid_spec,
        compiler_params=_cparams(("arbitrary", "arbitrary")),
        name="moe_down",
    )(tile_expert, n_used, act, wd, row_w)


def _norm_res_kernel(y0_ref, y1_ref, x_ref, g_ref, mod_ref, o_ref, *, tiles_per_batch, n_batch, gate_col):
    i = pl.program_id(0)
    grp = jnp.minimum(i // tiles_per_batch, n_batch)
    rows = 128
    _post_residual(lambda r0: y0_ref[pl.ds(r0, rows), :].astype(F32) + y1_ref[pl.ds(r0, rows), :].astype(F32),
                   x_ref, g_ref, mod_ref, grp, gate_col, o_ref, rows)


def _norm_residual(y0, y1, xs, g, mod, layer, *, gate_col, n_batch, seq, m_rows, in_place, tm=512):
    d = xs.shape[1]
    kern = functools.partial(_norm_res_kernel, tiles_per_batch=seq // tm, n_batch=n_batch, gate_col=gate_col)
    row = lambda: pl.BlockSpec((tm, d), lambda i: (i, 0))
    return pl.pallas_call(
        kern,
        out_shape=jax.ShapeDtypeStruct(xs.shape if in_place else (m_rows, d), F32),
        grid=(m_rows // tm,),
        in_specs=[row(), row(), row(),
                  pl.BlockSpec((None, 1, d), lambda i: (layer, 0, 0)),
                  pl.BlockSpec((None, MOD_ROWS, N_MOD * d), lambda i: (layer, 0, 0))],
        out_specs=row(),
        input_output_aliases={2: 0} if in_place else {},
        compiler_params=_cparams(("parallel",)),
        name="norm_residual",
    )(y0, y1, xs, g, mod)


def _route_tables(top_i, top_w, tm):
    n = top_i.shape[0]
    flat_e = top_i.reshape(-1)
    onehot = (jnp.arange(N_EXPERTS, dtype=jnp.int32)[:, None] == flat_e[None, :]).astype(jnp.int32)
    csum = jnp.cumsum(onehot, axis=1)
    rank = jnp.sum(csum * onehot, axis=0) - 1
    counts = csum[:, -1]
    padded = ((counts + tm - 1) // tm) * tm
    ends = jnp.cumsum(padded)
    starts = ends - padded
    pos = jnp.sum(starts[:, None] * onehot, axis=0) + rank
    r_pad = 2 * n + N_EXPERTS * tm
    n_tiles = r_pad // tm
    row_pair = jnp.full((r_pad,), 2 * n, jnp.int32).at[pos].set(jnp.arange(2 * n, dtype=jnp.int32))
    is_real = row_pair < 2 * n
    row_token = jnp.where(is_real, row_pair // 2, 0)
    row_w = jnp.where(is_real, top_w.reshape(-1)[jnp.minimum(row_pair, 2 * n - 1)], 0.0)
    n_used = (ends[-1] // tm).astype(jnp.int32)
    tile_start = jnp.arange(n_tiles, dtype=jnp.int32) * tm
    tile_expert = jnp.sum((tile_start[:, None] >= ends[None, :]).astype(jnp.int32), axis=1)
    last = jnp.minimum(tile_expert[jnp.maximum(n_used - 1, 0)], N_EXPERTS - 1)
    tile_expert = jnp.where(jnp.arange(n_tiles) < n_used, jnp.minimum(tile_expert, N_EXPERTS - 1), last)
    return pos.reshape(n, 2), row_token, row_w, tile_expert.astype(jnp.int32), n_used.reshape(1)


def kernel(x, c, ctx, c_ctx, w_mod, b_mod, g_mix_pre, g_mix_post, g_ffn_pre, g_ffn_post, w_in, rpb,
           w_attn_out, conv_dw, conv_db, conv_ln_g, conv_ln_b, w_conv_out, w_out, w_ff_gate, w_ff_up,
           w_ff_down, w_router, w_exp_gate, w_exp_up, w_exp_down):
    n_batch, seq, d = x.shape
    ctx_len = ctx.shape[1]
    depth = w_mod.shape[0]
    n_lat = n_batch * seq
    m_all = n_lat + n_batch * ctx_len
    assert seq == GRID_W * GRID_W and d == N_HEADS * HEAD_DIM and n_batch + 1 <= MOD_ROWS
    geo = dict(n_batch=n_batch, seq=seq)

    w_in_b = w_in.astype(BF16)
    w_ao_b = w_attn_out.astype(BF16)
    w_co_b = w_conv_out.astype(BF16)
    w_o_b = w_out.astype(BF16)
    w_fg_b, w_fu_b, w_fd_b = w_ff_gate.astype(BF16), w_ff_up.astype(BF16), w_ff_down.astype(BF16)
    w_router_hi, w_router_lo = _split_bf16(jnp.pad(w_router, ((0, 0), (0, 0), (0, 128 - N_EXPERTS))))

    vec = lambda a: a.reshape(depth, 1, -1)
    g_mix_pre, g_mix_post, g_ffn_pre, g_ffn_post = map(vec, (g_mix_pre, g_mix_post, g_ffn_pre, g_ffn_post))
    conv_db, conv_ln_g, conv_ln_b = map(vec, (conv_db, conv_ln_g, conv_ln_b))

    s = jnp.zeros((MOD_ROWS, d), F32).at[:n_batch].set(_silu(c)).at[n_batch].set(_silu(c_ctx))
    mod = _mod_table(s, w_mod, b_mod)
    slabs = _attn_bias_slabs(rpb)

    xs = jnp.concatenate([x.reshape(n_lat, d), ctx.reshape(n_batch * ctx_len, d)], axis=0)
    attn_w = N_HEADS * HEAD_DIM
    conv_w = conv_dw.shape[-1]
    u_col0 = 3 * attn_w
    ga_col0 = u_col0 + 2 * conv_w
    gc_col0 = ga_col0 + d

    for layer in range(depth):
        last = layer == depth - 1
        m_rows = n_lat if last else m_all
        j = layer // 2

        proj = _in_proj(xs, g_mix_pre, mod, w_in_b, layer, **geo)
        o_attn = _attention(proj, slabs, layer, ctx_len=ctx_len, with_ctx_queries=not last, **geo)
        h_conv = _conv_module(proj, conv_dw, conv_db, conv_ln_g, conv_ln_b, layer, ctx_len=ctx_len,
                              a_col=u_col0 // conv_w, g_col=u_col0 // conv_w + 1, **geo)
        y = _merge(o_attn, h_conv, w_ao_b, w_co_b, proj, layer, ga_col0=ga_col0, gc_col0=gc_col0)
        xs = _matmul_residual(y, w_o_b, xs, g_mix_post, mod, layer, gate_col=2, m_rows=m_rows,
                              w_index=layer, **geo)

        if layer % 2 == 0:
            act = _ffn_in(xs, g_ffn_pre, mod, w_fg_b, w_fu_b, layer, j, m_rows=m_rows, **geo)
            xs = _matmul_residual(act, w_fd_b, xs, g_ffn_post, mod, layer, gate_col=5, m_rows=m_rows,
                                  w_index=j, tk=2816, **geo)
        else:
            h, top_i, top_w = _router(xs, g_ffn_pre, mod, w_router_hi, w_router_lo, layer, j, m_rows=m_rows,
                                      **geo)
            pos, row_token, row_w, tile_expert, n_used = _route_tables(top_i[:, :2], top_w[:, :2], MOE_TM)
            x_sorted = jnp.take(h, row_token, axis=0)
            act = _moe_up(tile_expert, n_used, x_sorted, w_exp_gate, w_exp_up, j)
            y_sorted = _moe_down(tile_expert, n_used, act, w_exp_down, row_w[:, None], j)
            y0 = jnp.take(y_sorted, pos[:, 0], axis=0)
            y1 = jnp.take(y_sorted, pos[:, 1], axis=0)
            xs = _norm_residual(y0, y1, xs, g_ffn_post, mod, layer, gate_col=5, m_rows=m_rows,
                                in_place=not last, **geo)

    return xs[:n_lat].reshape(n_batch, seq, d)
```

```python
import functools

import numpy as np
import jax
import jax.numpy as jnp
from jax import lax
from jax.experimental import pallas as pl
from jax.experimental.pallas import tpu as pltpu

F32 = jnp.float32
BF16 = jnp.bfloat16

EPS = 1e-6
GRID_W = 64
N_HEADS = 16
HEAD_DIM = 128
CONV_K = 31
WIN_H = 8
WIN_W = 16
N_MOD = 6
N_EXPERTS = 8
MOD_ROWS = 8

ATT_QROWS = 4
ATT_KROWS = 12
ATT_HPB = 4
CONV_T = 256
CONV_HALO = 16
MOE_TM = 1024

VMEM_LIMIT = 56 * 1024 * 1024


def _cparams(sem):
    return pltpu.CompilerParams(dimension_semantics=sem, vmem_limit_bytes=VMEM_LIMIT)


def _silu(x):
    return x * jax.nn.sigmoid(x)


def _rms(x, g):
    return x * lax.rsqrt(jnp.mean(x * x, axis=-1, keepdims=True) + EPS) * g


def _mod_row(mod_ref, grp, col, d):
    return mod_ref[pl.ds(grp, 1), col * d:(col + 1) * d]


def _row_chunks(n_rows, rows, body):
    def step(r, carry):
        body(pl.multiple_of(r * rows, rows))
        return carry
    lax.fori_loop(0, n_rows // rows, step, 0)


def _prenorm_chunk(x, g, shift, scale1):
    return _rms(x, g) * scale1 + shift


def _prenorm_into(h_ref, x_ref, g_ref, mod_ref, grp, shift_col, scale_col, rows=128):
    d = x_ref.shape[-1]
    shift = _mod_row(mod_ref, grp, shift_col, d)
    scale1 = 1.0 + _mod_row(mod_ref, grp, scale_col, d)
    g = g_ref[...]

    def body(r0):
        x = x_ref[pl.ds(r0, rows), :]
        h_ref[pl.ds(r0, rows), :] = _prenorm_chunk(x, g, shift, scale1).astype(h_ref.dtype)

    _row_chunks(x_ref.shape[0], rows, body)


def _mod_kernel(s_ref, w_ref, b_ref, o_ref):
    o_ref[...] = jnp.dot(s_ref[...], w_ref[...], preferred_element_type=F32,
                         precision=lax.Precision.HIGHEST) + b_ref[...]


def _mod_table(s, w_mod, b_mod):
    depth, d, n = w_mod.shape
    tn = 1024
    return pl.pallas_call(
        _mod_kernel,
        out_shape=jax.ShapeDtypeStruct((depth, MOD_ROWS, n), F32),
        grid=(depth, n // tn),
        in_specs=[
            pl.BlockSpec((MOD_ROWS, d), lambda l, j: (0, 0)),
            pl.BlockSpec((None, d, tn), lambda l, j: (l, 0, j)),
            pl.BlockSpec((None, 1, tn), lambda l, j: (l, 0, j)),
        ],
        out_specs=pl.BlockSpec((None, MOD_ROWS, tn), lambda l, j: (l, 0, j)),
        compiler_params=_cparams(("parallel", "parallel")),
        name="mod_table",
    )(s, w_mod, b_mod.reshape(depth, 1, n))


def _in_proj_kernel(x_ref, g_ref, mod_ref, w_ref, o_ref, h_ref, *, tiles_per_batch, n_batch):
    i = pl.program_id(0)

    @pl.when(pl.program_id(1) == 0)
    def _():
        grp = jnp.minimum(i // tiles_per_batch, n_batch)
        _prenorm_into(h_ref, x_ref, g_ref, mod_ref, grp, 0, 1)

    o_ref[...] = jnp.dot(h_ref[...], w_ref[...], preferred_element_type=F32).astype(o_ref.dtype)


def _in_proj(xs, g, mod, w, layer, *, n_batch, seq, tm=1024, tn=1024):
    m, d = xs.shape
    n = w.shape[-1]
    kern = functools.partial(_in_proj_kernel, tiles_per_batch=seq // tm, n_batch=n_batch)
    return pl.pallas_call(
        kern,
        out_shape=jax.ShapeDtypeStruct((m, n), BF16),
        grid=(m // tm, n // tn),
        in_specs=[
            pl.BlockSpec((tm, d), lambda i, j: (i, 0)),
            pl.BlockSpec((None, 1, d), lambda i, j: (layer, 0, 0)),
            pl.BlockSpec((None, MOD_ROWS, N_MOD * d), lambda i, j: (layer, 0, 0)),
            pl.BlockSpec((None, d, tn), lambda i, j: (layer, 0, j)),
        ],
        out_specs=pl.BlockSpec((tm, tn), lambda i, j: (i, j)),
        scratch_shapes=[pltpu.VMEM((tm, d), BF16)],
        compiler_params=_cparams(("parallel", "arbitrary")),
        name="in_proj",
    )(xs, g, mod, w)


def _attn_tables():
    rows = GRID_W
    qi = np.arange(ATT_QROWS)[:, None]
    kr = np.arange(ATT_KROWS)[None, :]
    drs, row_oks = [], []
    for i0, start in ((0, 0), (2 * ATT_QROWS, ATT_QROWS), (rows - ATT_QROWS, rows - ATT_KROWS)):
        i = i0 + qi
        r = start + kr
        r0 = np.clip(i - WIN_H // 2, 0, rows - WIN_H)
        row_oks.append((r >= r0) & (r < r0 + WIN_H))
        drs.append(np.clip(r - i + (WIN_H - 1), 0, 2 * WIN_H - 2))
    qj = np.arange(GRID_W)[:, None]
    kc = np.arange(GRID_W)[None, :]
    ws = np.clip(qj - WIN_W // 2, 0, GRID_W - WIN_W)
    col_ok = (kc >= ws) & (kc < ws + WIN_W)
    dc = np.clip(kc - qj, -(WIN_W - 1), WIN_W - 1) + (WIN_W - 1)
    return np.stack(drs).astype(np.int32), np.stack(row_oks), dc.astype(np.int32), col_ok


def _attn_pair_plan():
    dr, row_ok, _, _ = _attn_tables()
    entries, plan = [], []
    for pat in range(dr.shape[0]):
        per_q = []
        for qi in range(ATT_QROWS):
            per_p = []
            for p in range(ATT_KROWS // 2):
                halves = tuple(int(dr[pat, qi, kr]) if row_ok[pat, qi, kr] else None for kr in (2 * p, 2 * p + 1))
                if halves == (None, None):
                    per_p.append(None)
                    continue
                if halves not in entries:
                    entries.append(halves)
                per_p.append(entries.index(halves))
            per_q.append(per_p)
        plan.append(per_q)
    return entries, plan


_LOG2E = float(np.log2(np.e))


def _attn_bias_slabs(rpb):
    _, _, dc, col_ok = _attn_tables()
    entries, _ = _attn_pair_plan()
    onehot = (dc[:, :, None] == np.arange(2 * WIN_W - 1)[None, None, :]).astype(np.float32)
    t = jnp.einsum("lhdk,jck->lhdjc", rpb.astype(F32), onehot, precision=lax.Precision.HIGHEST)
    t = jnp.where(col_ok, t * _LOG2E, -jnp.inf)
    masked = jnp.full(t.shape[:2] + t.shape[3:], -jnp.inf, F32)
    half = lambda d: masked if d is None else t[:, :, d]
    return jnp.stack([jnp.concatenate([half(a), half(b)], axis=-1) for a, b in entries], axis=2)


def _attn_kernel(q_ref, k_ref, v_ref, kc_ref, vc_ref, slab_ref, o_ref, bias_ref, *,
                 n_qblocks, max_start, with_ctx_queries):
    rb = pl.program_id(2)
    nk = ATT_KROWS * GRID_W
    scale2 = HEAD_DIM ** -0.5 * _LOG2E
    dn = (((1,), (1,)), ((), ()))
    _, plan = _attn_pair_plan()

    def scores(q, k):
        return lax.dot_general(q, k, dn, preferred_element_type=F32) * scale2

    def assemble(pattern):
        masked = jnp.full((GRID_W, 2 * GRID_W), -jnp.inf, F32)
        for hh in range(ATT_HPB):
            for qi in range(ATT_QROWS):
                for p, e in enumerate(plan[pattern][qi]):
                    blk = masked if e is None else slab_ref[hh, e]
                    bias_ref[hh, qi * GRID_W:(qi + 1) * GRID_W, 2 * p * GRID_W:2 * (p + 1) * GRID_W] = blk

    for pattern, first_rb in ((0, 0), (1, 1), (2, n_qblocks - 1)):
        pl.when(rb == first_rb)(functools.partial(assemble, pattern))

    @pl.when(rb < n_qblocks)
    def _():
        start = jnp.clip(rb * ATT_QROWS - WIN_H // 2, 0, max_start)
        tok0 = pl.multiple_of(start * GRID_W, GRID_W)
        heads = [slice(hh * HEAD_DIM, (hh + 1) * HEAD_DIM) for hh in range(ATT_HPB)]

        s_all = [(scores(q_ref[:, hs], k_ref[pl.ds(tok0, nk), hs]) + bias_ref[hh], scores(q_ref[:, hs], kc_ref[:, hs]))
                 for hh, hs in enumerate(heads)]
        for hs, (s_loc, s_ctx) in zip(heads, s_all):
            m = jnp.maximum(jnp.max(s_loc, axis=-1, keepdims=True), jnp.max(s_ctx, axis=-1, keepdims=True))
            p_loc = jnp.exp2(s_loc - m)
            p_ctx = jnp.exp2(s_ctx - m)
            l = jnp.sum(p_loc, axis=-1, keepdims=True) + jnp.sum(p_ctx, axis=-1, keepdims=True)
            o = (jnp.dot(p_loc.astype(BF16), v_ref[pl.ds(tok0, nk), hs], preferred_element_type=F32)
                 + jnp.dot(p_ctx.astype(BF16), vc_ref[:, hs], preferred_element_type=F32))
            o_ref[:, hs] = (o / l).astype(o_ref.dtype)

    if with_ctx_queries:
        @pl.when(rb == n_qblocks)
        def _():
            for hh in range(ATT_HPB):
                hs = slice(hh * HEAD_DIM, (hh + 1) * HEAD_DIM)
                s_ctx = scores(q_ref[:, hs], kc_ref[:, hs])
                p_ctx = jnp.exp2(s_ctx - jnp.max(s_ctx, axis=-1, keepdims=True))
                l = jnp.sum(p_ctx, axis=-1, keepdims=True)
                o = jnp.dot(p_ctx.astype(BF16), vc_ref[:, hs], preferred_element_type=F32)
                o_ref[:, hs] = (o / l).astype(o_ref.dtype)


def _attention(proj, slabs, layer, *, n_batch, seq, ctx_len, with_ctx_queries):
    m = proj.shape[0]
    nq = ATT_QROWS * GRID_W
    assert nq == ctx_len, "context queries reuse the latent query-block shape"
    rows = seq // GRID_W
    n_qblocks = rows // ATT_QROWS
    ctx_blk0 = (n_batch * seq) // ctx_len
    n_hg = N_HEADS // ATT_HPB
    wblk = ATT_HPB * HEAD_DIM

    def qrow(b, rb):
        return jnp.where(rb == n_qblocks, ctx_blk0 + b, b * n_qblocks + rb)

    assert n_qblocks >= 3, "first / interior / last bias patterns need distinct query blocks"
    n_entries = slabs.shape[2]
    kern = functools.partial(_attn_kernel, n_qblocks=n_qblocks, max_start=rows - ATT_KROWS,
                             with_ctx_queries=with_ctx_queries)
    return pl.pallas_call(
        kern,
        out_shape=jax.ShapeDtypeStruct((m, N_HEADS * HEAD_DIM), BF16),
        grid=(n_batch, n_hg, n_qblocks + (1 if with_ctx_queries else 0)),
        in_specs=[
            pl.BlockSpec((nq, wblk), lambda b, h, rb: (qrow(b, rb), h)),
            pl.BlockSpec((seq, wblk), lambda b, h, rb: (b, n_hg + h)),
            pl.BlockSpec((seq, wblk), lambda b, h, rb: (b, 2 * n_hg + h)),
            pl.BlockSpec((ctx_len, wblk), lambda b, h, rb: (ctx_blk0 + b, n_hg + h)),
            pl.BlockSpec((ctx_len, wblk), lambda b, h, rb: (ctx_blk0 + b, 2 * n_hg + h)),
            pl.BlockSpec((None, ATT_HPB, n_entries, GRID_W, 2 * GRID_W), lambda b, h, rb: (layer, h, 0, 0, 0)),
        ],
        out_specs=pl.BlockSpec((nq, wblk), lambda b, h, rb: (qrow(b, rb), h)),
        scratch_shapes=[pltpu.VMEM((ATT_HPB, nq, ATT_KROWS * GRID_W), F32)],
        compiler_params=_cparams(("parallel", "parallel", "arbitrary")),
        name="attention",
    )(proj, proj, proj, proj, proj, slabs)


def _conv_kernel(a_ref, g_ref, ap_ref, gp_ref, an_ref, gn_ref, w_ref, b_ref, lg_ref, lb_ref,
                 o_ref, hext_ref, y_ref, *, tiles_per_seq, n_latent_tiles):
    i = pl.program_id(0)
    t = CONV_T
    halo = CONV_HALO
    c = a_ref.shape[-1]
    is_ctx = i >= n_latent_tiles
    pos = i % tiles_per_seq
    at_start = jnp.logical_or(is_ctx, pos == 0)
    at_end = jnp.logical_or(is_ctx, pos == tiles_per_seq - 1)

    def glu(a, g):
        return a.astype(F32) * jax.nn.sigmoid(g.astype(F32))

    lanes = 128
    rchunk = 64
    off = halo - CONV_K // 2

    hext_ref[0:halo, :] = jnp.where(at_start, 0.0, glu(ap_ref[...], gp_ref[...]))
    hext_ref[halo + t:halo + t + halo, :] = jnp.where(at_end, 0.0, glu(an_ref[...], gn_ref[...]))

    def glu_body(r0):
        dst = pl.multiple_of(r0 + halo, halo)
        hext_ref[pl.ds(dst, rchunk), :] = glu(a_ref[pl.ds(r0, rchunk), :], g_ref[pl.ds(r0, rchunk), :])

    _row_chunks(t, rchunk, glu_body)

    def chunk_body(ci, carry):
        c0 = pl.multiple_of(ci * lanes, lanes)
        wts = w_ref[:, pl.ds(c0, lanes)]
        bias = b_ref[:, pl.ds(c0, lanes)]
        for r0 in range(0, t, rchunk):
            acc = bias
            for res in range(8):
                n_rows = rchunk + (8 if res else 0)
                part = None
                for base in range(0, CONV_K + off, 8):
                    k = base + res - off
                    if 0 <= k < CONV_K:
                        term = wts[k:k + 1, :] * hext_ref[r0 + base:r0 + base + n_rows, pl.ds(c0, lanes)]
                        part = term if part is None else part + term
                if part is not None:
                    acc = acc + part[res:res + rchunk, :]
            y_ref[r0:r0 + rchunk, pl.ds(c0, lanes)] = acc
        return carry

    lax.fori_loop(0, c // lanes, chunk_body, 0)

    ln_g = lg_ref[...]
    ln_b = lb_ref[...]

    def ln_body(r0):
        y = y_ref[pl.ds(r0, rchunk), :]
        mu = jnp.mean(y, axis=-1, keepdims=True)
        yc = y - mu
        var = jnp.mean(yc * yc, axis=-1, keepdims=True)
        z = yc * lax.rsqrt(var + EPS) * ln_g + ln_b
        o_ref[pl.ds(r0, rchunk), :] = _silu(z).astype(o_ref.dtype)

    _row_chunks(t, rchunk, ln_body)


def _conv_module(proj, w_dw, b_dw, ln_g, ln_b, layer, *, n_batch, seq, ctx_len, a_col, g_col):
    m = proj.shape[0]
    c = w_dw.shape[-1]
    t = CONV_T
    assert ctx_len == t and seq % t == 0
    hb = t // CONV_HALO
    last_hblk = m // CONV_HALO - 1
    n_tiles = m // t
    kern = functools.partial(_conv_kernel, tiles_per_seq=seq // t, n_latent_tiles=(n_batch * seq) // t)

    def prev(i):
        return jnp.maximum(i * hb - 1, 0)

    def nxt(i):
        return jnp.minimum(i * hb + hb, last_hblk)

    vec = lambda: pl.BlockSpec((None, 1, c), lambda i: (layer, 0, 0))
    return pl.pallas_call(
        kern,
        out_shape=jax.ShapeDtypeStruct((m, c), BF16),
        grid=(n_tiles,),
        in_specs=[
            pl.BlockSpec((t, c), lambda i: (i, a_col)),
            pl.BlockSpec((t, c), lambda i: (i, g_col)),
            pl.BlockSpec((CONV_HALO, c), lambda i: (prev(i), a_col)),
            pl.BlockSpec((CONV_HALO, c), lambda i: (prev(i), g_col)),
            pl.BlockSpec((CONV_HALO, c), lambda i: (nxt(i), a_col)),
            pl.BlockSpec((CONV_HALO, c), lambda i: (nxt(i), g_col)),
            pl.BlockSpec((None, CONV_K, c), lambda i: (layer, 0, 0)),
            vec(), vec(), vec(),
        ],
        out_specs=pl.BlockSpec((t, c), lambda i: (i, 0)),
        scratch_shapes=[pltpu.VMEM((t + 2 * CONV_HALO, c), F32), pltpu.VMEM((t, c), F32)],
        compiler_params=_cparams(("parallel",)),
        name="conv_module",
    )(proj, proj, proj, proj, proj, proj, w_dw, b_dw, ln_g, ln_b)


def _merge_kernel(oa_ref, hc_ref, wa_ref, wc_ref, ga_ref, gc_ref, o_ref):
    ya = jnp.dot(oa_ref[...], wa_ref[...], preferred_element_type=F32)
    yc = jnp.dot(hc_ref[...], wc_ref[...], preferred_element_type=F32)
    y = jax.nn.sigmoid(ga_ref[...].astype(F32)) * ya + jax.nn.sigmoid(gc_ref[...].astype(F32)) * yc
    o_ref[...] = y.astype(o_ref.dtype)


def _merge(o_attn, h_conv, wa, wc, proj, layer, *, ga_col0, gc_col0, tm=512, tn=1024):
    m, k = o_attn.shape
    n = wa.shape[-1]
    return pl.pallas_call(
        _merge_kernel,
        out_shape=jax.ShapeDtypeStruct((m, n), BF16),
        grid=(n // tn, m // tm),
        in_specs=[
            pl.BlockSpec((tm, k), lambda j, i: (i, 0)),
            pl.BlockSpec((tm, k), lambda j, i: (i, 0)),
            pl.BlockSpec((None, k, tn), lambda j, i: (layer, 0, j)),
            pl.BlockSpec((None, k, tn), lambda j, i: (layer, 0, j)),
            pl.BlockSpec((tm, tn), lambda j, i: (i, ga_col0 // tn + j)),
            pl.BlockSpec((tm, tn), lambda j, i: (i, gc_col0 // tn + j)),
        ],
        out_specs=pl.BlockSpec((tm, tn), lambda j, i: (i, j)),
        compiler_params=_cparams(("parallel", "parallel")),
        name="merge",
    )(o_attn, h_conv, wa, wc, proj, proj)


def _post_residual(z_chunk, x_ref, g_ref, mod_ref, grp, gate_col, o_ref, rows=128):
    d = x_ref.shape[-1]
    gate = _mod_row(mod_ref, grp, gate_col, d)
    g = g_ref[...]

    def body(r0):
        o_ref[pl.ds(r0, rows), :] = x_ref[pl.ds(r0, rows), :] + gate * _rms(z_chunk(r0), g)

    _row_chunks(x_ref.shape[0], rows, body)


def _matmul_res_kernel(a_ref, w_ref, x_ref, g_ref, mod_ref, o_ref, acc0_ref, acc1_ref, *,
                       tiles_per_batch, n_batch, gate_col, nk, n_tiles):
    i = pl.program_id(0)
    kk = pl.program_id(1)
    tm, d = x_ref.shape
    rows = tm // nk
    gate = _mod_row(mod_ref, jnp.minimum(jnp.maximum(i - 1, 0) // tiles_per_batch, n_batch), gate_col, d)

    @pl.when(jnp.logical_and(i == 0, kk == 0))
    def _():
        acc1_ref[...] = jnp.zeros_like(acc1_ref)

    for parity, (acc_w, acc_r) in enumerate(((acc0_ref, acc1_ref), (acc1_ref, acc0_ref))):
        for k_step in range(nk):
            @pl.when(jnp.logical_and(i % 2 == parity, kk == k_step))
            def _(acc_w=acc_w, acc_r=acc_r, k_step=k_step):
                part = jnp.dot(a_ref[...], w_ref[...], preferred_element_type=F32)
                if k_step == 0:
                    acc_w[...] = part
                else:
                    acc_w[...] += part
                sub = min(rows, 128)
                for r0 in range(k_step * rows, (k_step + 1) * rows, sub):
                    z = acc_r[r0:r0 + sub, :]
                    o_ref[r0:r0 + sub, :] = x_ref[r0:r0 + sub, :] + gate * _rms(z, g_ref[...])


def _matmul_residual(a, w, xs, g, mod, layer, *, gate_col, n_batch, seq, m_rows, w_index, tm=512, tk=None):
    k = a.shape[1]
    d = xs.shape[1]
    tk = k if tk is None else tk
    nk = k // tk
    n_tiles = m_rows // tm
    kern = functools.partial(_matmul_res_kernel, tiles_per_batch=seq // tm, n_batch=n_batch,
                             gate_col=gate_col, nk=nk, n_tiles=n_tiles)
    cur = lambda i: jnp.minimum(i, n_tiles - 1)
    prev = lambda i: jnp.maximum(i - 1, 0)
    return pl.pallas_call(
        kern,
        out_shape=jax.ShapeDtypeStruct(xs.shape, F32),
        grid=(n_tiles + 1, nk),
        in_specs=[
            pl.BlockSpec((tm, tk), lambda i, kk: (cur(i), kk)),
            pl.BlockSpec((None, tk, d), lambda i, kk: (w_index, kk, 0)),
            pl.BlockSpec((tm, d), lambda i, kk: (prev(i), 0)),
            pl.BlockSpec((None, 1, d), lambda i, kk: (layer, 0, 0)),
            pl.BlockSpec((None, MOD_ROWS, N_MOD * d), lambda i, kk: (layer, 0, 0)),
        ],
        out_specs=pl.BlockSpec((tm, d), lambda i, kk: (prev(i), 0)),
        scratch_shapes=[pltpu.VMEM((tm, d), F32), pltpu.VMEM((tm, d), F32)],
        input_output_aliases={2: 0},
        compiler_params=_cparams(("arbitrary", "arbitrary")),
        name="matmul_residual",
    )(a, w, xs, g, mod)


def _ffn_in_kernel(x_ref, g_ref, mod_ref, wg_ref, wu_ref, o_ref, h_ref, *, tiles_per_batch, n_batch):
    i = pl.program_id(0)

    @pl.when(pl.program_id(1) == 0)
    def _():
        grp = jnp.minimum(i // tiles_per_batch, n_batch)
        _prenorm_into(h_ref, x_ref, g_ref, mod_ref, grp, 3, 4)

    h = h_ref[...]
    a = jnp.dot(h, wg_ref[...], preferred_element_type=F32)
    u = jnp.dot(h, wu_ref[...], preferred_element_type=F32)
    o_ref[...] = (_silu(a) * u).astype(o_ref.dtype)


def _ffn_in(xs, g, mod, wg, wu, layer, j_dense, *, n_batch, seq, m_rows, tm=1024, tn=512):
    d = xs.shape[1]
    f = wg.shape[-1]
    kern = functools.partial(_ffn_in_kernel, tiles_per_batch=seq // tm, n_batch=n_batch)
    return pl.pallas_call(
        kern,
        out_shape=jax.ShapeDtypeStruct((m_rows, f), BF16),
        grid=(m_rows // tm, f // tn),
        in_specs=[
            pl.BlockSpec((tm, d), lambda i, j: (i, 0)),
            pl.BlockSpec((None, 1, d), lambda i, j: (layer, 0, 0)),
            pl.BlockSpec((None, MOD_ROWS, N_MOD * d), lambda i, j: (layer, 0, 0)),
            pl.BlockSpec((None, d, tn), lambda i, j: (j_dense, 0, j)),
            pl.BlockSpec((None, d, tn), lambda i, j: (j_dense, 0, j)),
        ],
        out_specs=pl.BlockSpec((tm, tn), lambda i, j: (i, j)),
        scratch_shapes=[pltpu.VMEM((tm, d), BF16)],
        compiler_params=_cparams(("parallel", "arbitrary")),
        name="ffn_in",
    )(xs, g, mod, wg, wu)


def _split_bf16(a):
    hi = a.astype(BF16)
    return hi, (a - hi.astype(F32)).astype(BF16)


def _router_kernel(x_ref, g_ref, mod_ref, wh_ref, wl_ref, h_ref, ti_ref, tw_ref, *, tiles_per_batch, n_batch):
    i = pl.program_id(0)
    grp = jnp.minimum(i // tiles_per_batch, n_batch)
    d = x_ref.shape[-1]
    shift = _mod_row(mod_ref, grp, 3, d)
    scale1 = 1.0 + _mod_row(mod_ref, grp, 4, d)
    g = g_ref[...]
    rows = 256

    def body(r0):
        h = _prenorm_chunk(x_ref[pl.ds(r0, rows), :], g, shift, scale1)
        h_ref[pl.ds(r0, rows), :] = h.astype(h_ref.dtype)
        h_hi, h_lo = _split_bf16(h)
        dot = functools.partial(jnp.dot, preferred_element_type=F32)
        logits = dot(h_hi, wh_ref[...]) + (dot(h_hi, wl_ref[...]) + dot(h_lo, wh_ref[...]))
        lane =lax.broadcasted_iota(jnp.int32, logits.shape, 1)
        big = jnp.int32(logits.shape[-1])
        logits = jnp.where(lane < N_EXPERTS, logits, -jnp.inf)
        m1 = jnp.max(logits, axis=-1, keepdims=True)
        i1 = jnp.min(jnp.where(logits == m1, lane, big), axis=-1, keepdims=True)
        rest = jnp.where(lane == i1, -jnp.inf, logits)
        m2 = jnp.max(rest, axis=-1, keepdims=True)
        i2 = jnp.min(jnp.where(rest == m2, lane, big), axis=-1, keepdims=True)
        e2 = jnp.exp(m2 - m1)
        w1 = 1.0 / (1.0 + e2)
        w2 = e2 / (1.0 + e2)
        ti_ref[pl.ds(r0, rows), :] = jnp.where(lane == 0, i1, jnp.where(lane == 1, i2, 0))
        tw_ref[pl.ds(r0, rows), :] = jnp.where(lane == 0, w1, jnp.where(lane == 1, w2, 0.0))

    _row_chunks(x_ref.shape[0], rows, body)


def _router(xs, g, mod, w_hi, w_lo, layer, j_moe, *, n_batch, seq, m_rows, tm=512):
    d = xs.shape[1]
    lanes = w_hi.shape[-1]
    kern = functools.partial(_router_kernel, tiles_per_batch=seq // tm, n_batch=n_batch)
    return pl.pallas_call(
        kern,
        out_shape=(jax.ShapeDtypeStruct((m_rows, d), BF16),
                   jax.ShapeDtypeStruct((m_rows, lanes), jnp.int32),
                   jax.ShapeDtypeStruct((m_rows, lanes), F32)),
        grid=(m_rows // tm,),
        in_specs=[
            pl.BlockSpec((tm, d), lambda i: (i, 0)),
            pl.BlockSpec((None, 1, d), lambda i: (layer, 0, 0)),
            pl.BlockSpec((None, MOD_ROWS, N_MOD * d), lambda i: (layer, 0, 0)),
            pl.BlockSpec((None, d, lanes), lambda i: (j_moe, 0, 0)),
            pl.BlockSpec((None, d, lanes), lambda i: (j_moe, 0, 0)),
        ],
        out_specs=(pl.BlockSpec((tm, d), lambda i: (i, 0)),
                   pl.BlockSpec((tm, lanes), lambda i: (i, 0)),
                   pl.BlockSpec((tm, lanes), lambda i: (i, 0))),
        compiler_params=_cparams(("parallel",)),
        name="router",
    )(xs, g, mod, w_hi, w_lo)


def _expert_block_is_new(te_ref, t):
    return jnp.logical_or(t == 0, te_ref[t] != te_ref[jnp.maximum(t - 1, 0)])


def _moe_up_kernel(te_ref, nu_ref, x_ref, wg_ref, wu_ref, o_ref, wgb_ref, wub_ref):
    t = pl.program_id(1)

    @pl.when(_expert_block_is_new(te_ref, t))
    def _():
        wgb_ref[...] = wg_ref[...].astype(BF16)
        wub_ref[...] = wu_ref[...].astype(BF16)

    @pl.when(t < nu_ref[0])
    def _():
        x = x_ref[...]
        a = jnp.dot(x, wgb_ref[...], preferred_element_type=F32)
        u = jnp.dot(x, wub_ref[...], preferred_element_type=F32)
        o_ref[...] = (_silu(a) * u).astype(o_ref.dtype)

    @pl.when(t >= nu_ref[0])
    def _():
        o_ref[...] = jnp.zeros_like(o_ref)


def _moe_up(tile_expert, n_used, x_sorted, wg, wu, j_moe, *, tn=512):
    r, d = x_sorted.shape
    f = wg.shape[-1]
    tm = MOE_TM
    grid_spec = pltpu.PrefetchScalarGridSpec(
        num_scalar_prefetch=2,
        grid=(f // tn, r // tm),
        in_specs=[
            pl.BlockSpec((tm, d), lambda j, t, te, nu: (t, 0)),
            pl.BlockSpec((None, None, d, tn), lambda j, t, te, nu: (j_moe, te[t], 0, j)),
            pl.BlockSpec((None, None, d, tn), lambda j, t, te, nu: (j_moe, te[t], 0, j)),
        ],
        out_specs=pl.BlockSpec((tm, tn), lambda j, t, te, nu: (t, j)),
        scratch_shapes=[pltpu.VMEM((d, tn), BF16), pltpu.VMEM((d, tn), BF16)],
    )
    return pl.pallas_call(
        _moe_up_kernel,
        out_shape=jax.ShapeDtypeStruct((r, f), BF16),
        grid_spec=grid_spec,
        compiler_params=_cparams(("arbitrary", "arbitrary")),
        name="moe_up",
    )(tile_expert, n_used, x_sorted, wg, wu)


def _moe_down_kernel(te_ref, nu_ref, a_ref, wd_ref, rw_ref, o_ref, wdb_ref):
    t = pl.program_id(1)

    @pl.when(_expert_block_is_new(te_ref, t))
    def _():
        wdb_ref[...] = wd_ref[...].astype(BF16)

    @pl.when(t < nu_ref[0])
    def _():
        y = jnp.dot(a_ref[...], wdb_ref[...], preferred_element_type=F32)
        o_ref[...] = (y * rw_ref[...]).astype(o_ref.dtype)

    @pl.when(t >= nu_ref[0])
    def _():
        o_ref[...] = jnp.zeros_like(o_ref)


def _moe_down(tile_expert, n_used, act, wd, row_w, j_moe, *, tn=512):
    r, f = act.shape
    d = wd.shape[-1]
    tm = MOE_TM
    grid_spec = pltpu.PrefetchScalarGridSpec(
        num_scalar_prefetch=2,
        grid=(d // tn, r // tm),
        in_specs=[
            pl.BlockSpec((tm, f), lambda j, t, te, nu: (t, 0)),
            pl.BlockSpec((None, None, f, tn), lambda j, t, te, nu: (j_moe, te[t], 0, j)),
            pl.BlockSpec((tm, 1), lambda j, t, te, nu: (t, 0)),
        ],
        out_specs=pl.BlockSpec((tm, tn), lambda j, t, te, nu: (t, j)),
        scratch_shapes=[pltpu.VMEM((f, tn), BF16)],
    )
    return pl.pallas_call(
        _moe_down_kernel,
        out_shape=jax.ShapeDtypeStruct((r, d), BF16),
        grid_spec=grid_spec,
        compiler_params=_cparams(("arbitrary", "arbitrary")),
        name="moe_down",
    )(tile_expert, n_used, act, wd, row_w)


def _norm_res_kernel(y0_ref, y1_ref, x_ref, g_ref, mod_ref, o_ref, *, tiles_per_batch, n_batch, gate_col):
    i = pl.program_id(0)
    grp = jnp.minimum(i // tiles_per_batch, n_batch)
    rows = 128
    _post_residual(lambda r0: y0_ref[pl.ds(r0, rows), :].astype(F32) + y1_ref[pl.ds(r0, rows), :].astype(F32),
                   x_ref, g_ref, mod_ref, grp, gate_col, o_ref, rows)


def _norm_residual(y0, y1, xs, g, mod, layer, *, gate_col, n_batch, seq, m_rows, in_place, tm=512):
    d = xs.shape[1]
    kern = functools.partial(_norm_res_kernel, tiles_per_batch=seq // tm, n_batch=n_batch, gate_col=gate_col)
    row = lambda: pl.BlockSpec((tm, d), lambda i: (i, 0))
    return pl.pallas_call(
        kern,
        out_shape=jax.ShapeDtypeStruct(xs.shape if in_place else (m_rows, d), F32),
        grid=(m_rows // tm,),
        in_specs=[row(), row(), row(),
                  pl.BlockSpec((None, 1, d), lambda i: (layer, 0, 0)),
                  pl.BlockSpec((None, MOD_ROWS, N_MOD * d), lambda i: (layer, 0, 0))],
        out_specs=row(),
        input_output_aliases={2: 0} if in_place else {},
        compiler_params=_cparams(("parallel",)),
        name="norm_residual",
    )(y0, y1, xs, g, mod)


RANK_CHUNK = 512


def _rank_kernel(e_ref, tri_ref, rank_ref, cnt_ref, carry_ref):
    @pl.when(pl.program_id(0) == 0)
    def _():
        carry_ref[...] = jnp.zeros_like(carry_ref)

    e = e_ref[...]
    expert = lax.broadcasted_iota(jnp.int32, (N_EXPERTS, e.shape[1]), 0)
    onehot = (expert == e).astype(F32)
    local = jnp.dot(onehot.astype(BF16), tri_ref[...], preferred_element_type=F32)
    carry = carry_ref[...]
    csum = local + carry[:, :1]
    rank_ref[...] = (jnp.sum(csum * onehot, axis=0, keepdims=True) - 1.0).astype(jnp.int32)
    carry = carry + jnp.sum(onehot, axis=1, keepdims=True)
    carry_ref[...] = carry
    cnt_ref[...] = carry.astype(jnp.int32)


def _pair_ranks(flat_e):
    n2 = flat_e.shape[0]
    c = RANK_CHUNK
    tri = jnp.asarray(np.triu(np.ones((c, c), np.float32)), BF16)
    rank, cnt = pl.pallas_call(
        _rank_kernel,
        out_shape=(jax.ShapeDtypeStruct((1, n2), jnp.int32), jax.ShapeDtypeStruct((N_EXPERTS, 128), jnp.int32)),
        grid=(n2 // c,),
        in_specs=[pl.BlockSpec((1, c), lambda s: (0, s)), pl.BlockSpec((c, c), lambda s: (0, 0))],
        out_specs=(pl.BlockSpec((1, c), lambda s: (0, s)), pl.BlockSpec((N_EXPERTS, 128), lambda s: (0, 0))),
        scratch_shapes=[pltpu.VMEM((N_EXPERTS, 128), F32)],
        compiler_params=_cparams(("arbitrary",)),
        name="pair_ranks",
    )(flat_e.reshape(1, n2), tri)
    return rank.reshape(n2), cnt[:, 0]


def _route_tables(top_i, top_w, tm):
    n = top_i.shape[0]
    flat_e = top_i.reshape(-1)
    rank, counts = _pair_ranks(flat_e)
    padded = ((counts + tm - 1) // tm) * tm
    ends = jnp.cumsum(padded)
    starts = ends - padded
    pos = starts[flat_e] + rank
    r_pad = 2 * n + N_EXPERTS * tm
    n_tiles = r_pad // tm
    row_pair = jnp.full((r_pad,), 2 * n, jnp.int32).at[pos].set(jnp.arange(2 * n, dtype=jnp.int32))
    is_real = row_pair < 2 * n
    spread = jnp.arange(r_pad, dtype=jnp.int32) % (2 * n)
    row_pair = jnp.where(is_real, row_pair, spread)
    row_token = row_pair // 2
    row_w = jnp.where(is_real, top_w.reshape(-1)[row_pair], 0.0)
    n_used = (ends[-1] // tm).astype(jnp.int32)
    tile_start = jnp.arange(n_tiles, dtype=jnp.int32) * tm
    tile_expert = jnp.sum((tile_start[:, None] >= ends[None, :]).astype(jnp.int32), axis=1)
    last = jnp.minimum(tile_expert[jnp.maximum(n_used - 1, 0)], N_EXPERTS - 1)
    tile_expert = jnp.where(jnp.arange(n_tiles) < n_used, jnp.minimum(tile_expert, N_EXPERTS - 1), last)
    return pos.reshape(n, 2), row_token, row_w, tile_expert.astype(jnp.int32), n_used.reshape(1)


def kernel(x, c, ctx, c_ctx, w_mod, b_mod, g_mix_pre, g_mix_post, g_ffn_pre, g_ffn_post, w_in, rpb,
           w_attn_out, conv_dw, conv_db, conv_ln_g, conv_ln_b, w_conv_out, w_out, w_ff_gate, w_ff_up,
           w_ff_down, w_router, w_exp_gate, w_exp_up, w_exp_down):
    n_batch, seq, d = x.shape
    ctx_len = ctx.shape[1]
    depth = w_mod.shape[0]
    n_lat = n_batch * seq
    m_all = n_lat + n_batch * ctx_len
    assert seq == GRID_W * GRID_W and d == N_HEADS * HEAD_DIM and n_batch + 1 <= MOD_ROWS
    geo = dict(n_batch=n_batch, seq=seq)

    w_in_b = w_in.astype(BF16)
    w_ao_b = w_attn_out.astype(BF16)
    w_co_b = w_conv_out.astype(BF16)
    w_o_b = w_out.astype(BF16)
    w_fg_b, w_fu_b, w_fd_b = w_ff_gate.astype(BF16), w_ff_up.astype(BF16), w_ff_down.astype(BF16)
    w_router_hi, w_router_lo = _split_bf16(jnp.pad(w_router, ((0, 0), (0, 0), (0, 128 - N_EXPERTS))))

    vec = lambda a: a.reshape(depth, 1, -1)
    g_mix_pre, g_mix_post, g_ffn_pre, g_ffn_post = map(vec, (g_mix_pre, g_mix_post, g_ffn_pre, g_ffn_post))
    conv_db, conv_ln_g, conv_ln_b = map(vec, (conv_db, conv_ln_g, conv_ln_b))

    s = jnp.zeros((MOD_ROWS, d), F32).at[:n_batch].set(_silu(c)).at[n_batch].set(_silu(c_ctx))
    mod = _mod_table(s, w_mod, b_mod)
    slabs = _attn_bias_slabs(rpb)

    xs = jnp.concatenate([x.reshape(n_lat, d), ctx.reshape(n_batch * ctx_len, d)], axis=0)
    attn_w = N_HEADS * HEAD_DIM
    conv_w = conv_dw.shape[-1]
    u_col0 = 3 * attn_w
    ga_col0 = u_col0 + 2 * conv_w
    gc_col0 = ga_col0 + d

    for layer in range(depth):
        last = layer == depth - 1
        m_rows = n_lat if last else m_all
        j = layer // 2

        proj = _in_proj(xs, g_mix_pre, mod, w_in_b, layer, **geo)
        o_attn = _attention(proj, slabs, layer, ctx_len=ctx_len, with_ctx_queries=not last, **geo)
        h_conv = _conv_module(proj, conv_dw, conv_db, conv_ln_g, conv_ln_b, layer, ctx_len=ctx_len,
                              a_col=u_col0 // conv_w, g_col=u_col0 // conv_w + 1, **geo)
        y = _merge(o_attn, h_conv, w_ao_b, w_co_b, proj, layer, ga_col0=ga_col0, gc_col0=gc_col0)
        xs = _matmul_residual(y, w_o_b, xs, g_mix_post, mod, layer, gate_col=2, m_rows=m_rows,
                              w_index=layer, **geo)

        if layer % 2 == 0:
            act = _ffn_in(xs, g_ffn_pre, mod, w_fg_b, w_fu_b, layer, j, m_rows=m_rows, **geo)
            xs = _matmul_residual(act, w_fd_b, xs, g_ffn_post, mod, layer, gate_col=5, m_rows=m_rows,
                                  w_index=j, tk=2816, **geo)
        else:
            h, top_i, top_w = _router(xs, g_ffn_pre, mod, w_router_hi, w_router_lo, layer, j, m_rows=m_rows,
                                      **geo)
            pos, row_token, row_w, tile_expert, n_used = _route_tables(top_i[:, :2], top_w[:, :2], MOE_TM)
            x_sorted = jnp.take(h, row_token, axis=0)
            act = _moe_up(tile_expert, n_used, x_sorted, w_exp_gate, w_exp_up, j)
            y_sorted = _moe_down(tile_expert, n_used, act, w_exp_down, row_w[:, None], j)
            y0 = jnp.take(y_sorted, pos[:, 0], axis=0)
            y1 = jnp.take(y_sorted, pos[:, 1], axis=0)
            xs = _norm_residual(y0, y1, xs, g_ffn_post, mod, layer, gate_col=5, m_rows=m_rows,
                                in_place=not last, **geo)

    return xs[:n_lat].reshape(n_batch, seq, d)
```

```python
import functools

import numpy as np
import jax
import jax.numpy as jnp
from jax import lax
from jax.experimental import pallas as pl
from jax.experimental.pallas import tpu as pltpu

F32 = jnp.float32
BF16 = jnp.bfloat16

EPS = 1e-6
GRID_W = 64
N_HEADS = 16
HEAD_DIM = 128
CONV_K = 31
WIN_H = 8
WIN_W = 16
N_MOD = 6
N_EXPERTS = 8
MOD_ROWS = 8
ROUTE_ROWS = 8

ATT_QROWS = 4
ATT_KROWS = 12
ATT_HPB = 4
CONV_T = 256
CONV_HALO = 16
MOE_TM = 1024

VMEM_LIMIT = 56 * 1024 * 1024


def _cparams(sem):
    return pltpu.CompilerParams(dimension_semantics=sem, vmem_limit_bytes=VMEM_LIMIT)


def _silu(x):
    return x * jax.nn.sigmoid(x)


def _rms(x, g):
    return x * lax.rsqrt(jnp.mean(x * x, axis=-1, keepdims=True) + EPS) * g


def _mod_row(mod_ref, grp, col, d):
    return mod_ref[pl.ds(grp, 1), col * d:(col + 1) * d]


def _row_chunks(n_rows, rows, body):
    def step(r, carry):
        body(pl.multiple_of(r * rows, rows))
        return carry
    lax.fori_loop(0, n_rows // rows, step, 0)


def _prenorm_chunk(x, g, shift, scale1):
    return _rms(x, g) * scale1 + shift


def _prenorm_into(h_ref, x_ref, g_ref, mod_ref, grp, shift_col, scale_col, rows=128):
    d = x_ref.shape[-1]
    shift = _mod_row(mod_ref, grp, shift_col, d)
    scale1 = 1.0 + _mod_row(mod_ref, grp, scale_col, d)
    g = g_ref[...]

    def body(r0):
        x = x_ref[pl.ds(r0, rows), :]
        h_ref[pl.ds(r0, rows), :] = _prenorm_chunk(x, g, shift, scale1).astype(h_ref.dtype)

    _row_chunks(x_ref.shape[0], rows, body)


def _mod_kernel(s_ref, w_ref, b_ref, o_ref):
    o_ref[...] = jnp.dot(s_ref[...], w_ref[...], preferred_element_type=F32,
                         precision=lax.Precision.HIGHEST) + b_ref[...]


def _mod_table(s, w_mod, b_mod):
    depth, d, n = w_mod.shape
    tn = 1024
    return pl.pallas_call(
        _mod_kernel,
        out_shape=jax.ShapeDtypeStruct((depth, MOD_ROWS, n), F32),
        grid=(depth, n // tn),
        in_specs=[
            pl.BlockSpec((MOD_ROWS, d), lambda l, j: (0, 0)),
            pl.BlockSpec((None, d, tn), lambda l, j: (l, 0, j)),
            pl.BlockSpec((None, 1, tn), lambda l, j: (l, 0, j)),
        ],
        out_specs=pl.BlockSpec((None, MOD_ROWS, tn), lambda l, j: (l, 0, j)),
        compiler_params=_cparams(("parallel", "parallel")),
        name="mod_table",
    )(s, w_mod, b_mod.reshape(depth, 1, n))


def _in_proj_kernel(x_ref, g_ref, mod_ref, w_ref, o_ref, h_ref, *, tiles_per_batch, n_batch):
    i = pl.program_id(0)

    @pl.when(pl.program_id(1) == 0)
    def _():
        grp = jnp.minimum(i // tiles_per_batch, n_batch)
        _prenorm_into(h_ref, x_ref, g_ref, mod_ref, grp, 0, 1)

    o_ref[...] = jnp.dot(h_ref[...], w_ref[...], preferred_element_type=F32).astype(o_ref.dtype)


def _in_proj(xs, g, mod, w, layer, *, n_batch, seq, tm=1024, tn=2048):
    m, d = xs.shape
    n = w.shape[-1]
    kern = functools.partial(_in_proj_kernel, tiles_per_batch=seq // tm, n_batch=n_batch)
    return pl.pallas_call(
        kern,
        out_shape=jax.ShapeDtypeStruct((m, n), BF16),
        grid=(m // tm, n // tn),
        in_specs=[
            pl.BlockSpec((tm, d), lambda i, j: (i, 0)),
            pl.BlockSpec((None, 1, d), lambda i, j: (layer, 0, 0)),
            pl.BlockSpec((None, MOD_ROWS, N_MOD * d), lambda i, j: (layer, 0, 0)),
            pl.BlockSpec((None, d, tn), lambda i, j: (layer, 0, j)),
        ],
        out_specs=pl.BlockSpec((tm, tn), lambda i, j: (i, j)),
        scratch_shapes=[pltpu.VMEM((tm, d), BF16)],
        compiler_params=_cparams(("parallel", "arbitrary")),
        name="in_proj",
    )(xs, g, mod, w)


def _attn_tables():
    rows = GRID_W
    qi = np.arange(ATT_QROWS)[:, None]
    kr = np.arange(ATT_KROWS)[None, :]
    drs, row_oks = [], []
    for i0, start in ((0, 0), (2 * ATT_QROWS, ATT_QROWS), (rows - ATT_QROWS, rows - ATT_KROWS)):
        i = i0 + qi
        r = start + kr
        r0 = np.clip(i - WIN_H // 2, 0, rows - WIN_H)
        row_oks.append((r >= r0) & (r < r0 + WIN_H))
        drs.append(np.clip(r - i + (WIN_H - 1), 0, 2 * WIN_H - 2))
    qj = np.arange(GRID_W)[:, None]
    kc = np.arange(GRID_W)[None, :]
    ws = np.clip(qj - WIN_W // 2, 0, GRID_W - WIN_W)
    col_ok = (kc >= ws) & (kc < ws + WIN_W)
    dc = np.clip(kc - qj, -(WIN_W - 1), WIN_W - 1) + (WIN_W - 1)
    return np.stack(drs).astype(np.int32), np.stack(row_oks), dc.astype(np.int32), col_ok


def _attn_pair_plan():
    dr, row_ok, _, _ = _attn_tables()
    entries, plan = [], []
    for pat in range(dr.shape[0]):
        per_q = []
        for qi in range(ATT_QROWS):
            per_p = []
            for p in range(ATT_KROWS // 2):
                halves = tuple(int(dr[pat, qi, kr]) if row_ok[pat, qi, kr] else None for kr in (2 * p, 2 * p + 1))
                if halves == (None, None):
                    per_p.append(None)
                    continue
                if halves not in entries:
                    entries.append(halves)
                per_p.append(entries.index(halves))
            per_q.append(per_p)
        plan.append(per_q)
    return entries, plan


_LOG2E = float(np.log2(np.e))


def _attn_bias_slabs(rpb):
    _, _, dc, col_ok = _attn_tables()
    entries, _ = _attn_pair_plan()
    onehot = (dc[:, :, None] == np.arange(2 * WIN_W - 1)[None, None, :]).astype(np.float32)
    t = jnp.einsum("lhdk,jck->lhdjc", rpb.astype(F32), onehot, precision=lax.Precision.HIGHEST)
    t = jnp.where(col_ok, t * _LOG2E, -jnp.inf)
    masked = jnp.full(t.shape[:2] + t.shape[3:], -jnp.inf, F32)
    half = lambda d: masked if d is None else t[:, :, d]
    return jnp.stack([jnp.concatenate([half(a), half(b)], axis=-1) for a, b in entries], axis=2)


def _attn_kernel(q_ref, k_ref, v_ref, kc_ref, vc_ref, slab_ref, o_ref, bias_ref, *,
                 n_qblocks, max_start, with_ctx_queries):
    rb = pl.program_id(2)
    nk = ATT_KROWS * GRID_W
    scale2 = HEAD_DIM ** -0.5 * _LOG2E
    dn = (((1,), (1,)), ((), ()))
    _, plan = _attn_pair_plan()

    def scores(q, k):
        return lax.dot_general(q, k, dn, preferred_element_type=F32) * scale2

    def assemble(pattern):
        masked = jnp.full((GRID_W, 2 * GRID_W), -jnp.inf, F32)
        for hh in range(ATT_HPB):
            for qi in range(ATT_QROWS):
                for p, e in enumerate(plan[pattern][qi]):
                    blk = masked if e is None else slab_ref[hh, e]
                    bias_ref[hh, qi * GRID_W:(qi + 1) * GRID_W, 2 * p * GRID_W:2 * (p + 1) * GRID_W] = blk

    for pattern, first_rb in ((0, 0), (1, 1), (2, n_qblocks - 1)):
        pl.when(rb == first_rb)(functools.partial(assemble, pattern))

    @pl.when(rb < n_qblocks)
    def _():
        start = jnp.clip(rb * ATT_QROWS - WIN_H // 2, 0, max_start)
        tok0 = pl.multiple_of(start * GRID_W, GRID_W)
        heads = [slice(hh * HEAD_DIM, (hh + 1) * HEAD_DIM) for hh in range(ATT_HPB)]

        s_all = [(scores(q_ref[:, hs], k_ref[pl.ds(tok0, nk), hs]) + bias_ref[hh], scores(q_ref[:, hs], kc_ref[:, hs]))
                 for hh, hs in enumerate(heads)]
        for hs, (s_loc, s_ctx) in zip(heads, s_all):
            m = jnp.maximum(jnp.max(s_loc, axis=-1, keepdims=True), jnp.max(s_ctx, axis=-1, keepdims=True))
            p_loc = jnp.exp2(s_loc - m)
            p_ctx = jnp.exp2(s_ctx - m)
            l = jnp.sum(p_loc, axis=-1, keepdims=True) + jnp.sum(p_ctx, axis=-1, keepdims=True)
            o = (jnp.dot(p_loc.astype(BF16), v_ref[pl.ds(tok0, nk), hs], preferred_element_type=F32)
                 + jnp.dot(p_ctx.astype(BF16), vc_ref[:, hs], preferred_element_type=F32))
            o_ref[:, hs] = (o / l).astype(o_ref.dtype)

    if with_ctx_queries:
        @pl.when(rb == n_qblocks)
        def _():
            for hh in range(ATT_HPB):
                hs = slice(hh * HEAD_DIM, (hh + 1) * HEAD_DIM)
                s_ctx = scores(q_ref[:, hs], kc_ref[:, hs])
                p_ctx = jnp.exp2(s_ctx - jnp.max(s_ctx, axis=-1, keepdims=True))
                l = jnp.sum(p_ctx, axis=-1, keepdims=True)
                o = jnp.dot(p_ctx.astype(BF16), vc_ref[:, hs], preferred_element_type=F32)
                o_ref[:, hs] = (o / l).astype(o_ref.dtype)


def _attention(proj, slabs, layer, *, n_batch, seq, ctx_len, with_ctx_queries):
    m = proj.shape[0]
    nq = ATT_QROWS * GRID_W
    assert nq == ctx_len, "context queries reuse the latent query-block shape"
    rows = seq // GRID_W
    n_qblocks = rows // ATT_QROWS
    ctx_blk0 = (n_batch * seq) // ctx_len
    n_hg = N_HEADS // ATT_HPB
    wblk = ATT_HPB * HEAD_DIM

    def qrow(b, rb):
        return jnp.where(rb == n_qblocks, ctx_blk0 + b, b * n_qblocks + rb)

    assert n_qblocks >= 3, "first / interior / last bias patterns need distinct query blocks"
    n_entries = slabs.shape[2]
    kern = functools.partial(_attn_kernel, n_qblocks=n_qblocks, max_start=rows - ATT_KROWS,
                             with_ctx_queries=with_ctx_queries)
    return pl.pallas_call(
        kern,
        out_shape=jax.ShapeDtypeStruct((m, N_HEADS * HEAD_DIM), BF16),
        grid=(n_batch, n_hg, n_qblocks + (1 if with_ctx_queries else 0)),
        in_specs=[
            pl.BlockSpec((nq, wblk), lambda b, h, rb: (qrow(b, rb), h)),
            pl.BlockSpec((seq, wblk), lambda b, h, rb: (b, n_hg + h)),
            pl.BlockSpec((seq, wblk), lambda b, h, rb: (b, 2 * n_hg + h)),
            pl.BlockSpec((ctx_len, wblk), lambda b, h, rb: (ctx_blk0 + b, n_hg + h)),
            pl.BlockSpec((ctx_len, wblk), lambda b, h, rb: (ctx_blk0 + b, 2 * n_hg + h)),
            pl.BlockSpec((None, ATT_HPB, n_entries, GRID_W, 2 * GRID_W), lambda b, h, rb: (layer, h, 0, 0, 0)),
        ],
        out_specs=pl.BlockSpec((nq, wblk), lambda b, h, rb: (qrow(b, rb), h)),
        scratch_shapes=[pltpu.VMEM((ATT_HPB, nq, ATT_KROWS * GRID_W), F32)],
        compiler_params=_cparams(("parallel", "parallel", "arbitrary")),
        name="attention",
    )(proj, proj, proj, proj, proj, slabs)


def _conv_kernel(a_ref, g_ref, ap_ref, gp_ref, an_ref, gn_ref, w_ref, b_ref, lg_ref, lb_ref,
                 o_ref, hext_ref, y_ref, *, tiles_per_seq, n_latent_tiles):
    i = pl.program_id(0)
    t = CONV_T
    halo = CONV_HALO
    c = a_ref.shape[-1]
    is_ctx = i >= n_latent_tiles
    pos = i % tiles_per_seq
    at_start = jnp.logical_or(is_ctx, pos == 0)
    at_end = jnp.logical_or(is_ctx, pos == tiles_per_seq - 1)

    def glu(a, g):
        return a.astype(F32) * jax.nn.sigmoid(g.astype(F32))

    lanes = 128
    rchunk = 64
    off = halo - CONV_K // 2

    hext_ref[0:halo, :] = jnp.where(at_start, 0.0, glu(ap_ref[...], gp_ref[...]))
    hext_ref[halo + t:halo + t + halo, :] = jnp.where(at_end, 0.0, glu(an_ref[...], gn_ref[...]))

    def glu_body(r0):
        dst = pl.multiple_of(r0 + halo, halo)
        hext_ref[pl.ds(dst, rchunk), :] = glu(a_ref[pl.ds(r0, rchunk), :], g_ref[pl.ds(r0, rchunk), :])

    _row_chunks(t, rchunk, glu_body)

    def chunk_body(ci, carry):
        c0 = pl.multiple_of(ci * lanes, lanes)
        wts = w_ref[:, pl.ds(c0, lanes)]
        bias = b_ref[:, pl.ds(c0, lanes)]
        for r0 in range(0, t, rchunk):
            acc = bias
            for res in range(8):
                n_rows = rchunk + (8 if res else 0)
                part = None
                for base in range(0, CONV_K + off, 8):
                    k = base + res - off
                    if 0 <= k < CONV_K:
                        term = wts[k:k + 1, :] * hext_ref[r0 + base:r0 + base + n_rows, pl.ds(c0, lanes)]
                        part = term if part is None else part + term
                if part is not None:
                    acc = acc + part[res:res + rchunk, :]
            y_ref[r0:r0 + rchunk, pl.ds(c0, lanes)] = acc
        return carry

    lax.fori_loop(0, c // lanes, chunk_body, 0)

    ln_g = lg_ref[...]
    ln_b = lb_ref[...]

    def ln_body(r0):
        y = y_ref[pl.ds(r0, rchunk), :]
        mu = jnp.mean(y, axis=-1, keepdims=True)
        yc = y - mu
        var = jnp.mean(yc * yc, axis=-1, keepdims=True)
        z = yc * lax.rsqrt(var + EPS) * ln_g + ln_b
        o_ref[pl.ds(r0, rchunk), :] = _silu(z).astype(o_ref.dtype)

    _row_chunks(t, rchunk, ln_body)


def _conv_module(proj, w_dw, b_dw, ln_g, ln_b, layer, *, n_batch, seq, ctx_len, a_col, g_col):
    m = proj.shape[0]
    c = w_dw.shape[-1]
    t = CONV_T
    assert ctx_len == t and seq % t == 0
    hb = t // CONV_HALO
    last_hblk = m // CONV_HALO - 1
    n_tiles = m // t
    kern = functools.partial(_conv_kernel, tiles_per_seq=seq // t, n_latent_tiles=(n_batch * seq) // t)

    def prev(i):
        return jnp.maximum(i * hb - 1, 0)

    def nxt(i):
        return jnp.minimum(i * hb + hb, last_hblk)

    vec = lambda: pl.BlockSpec((None, 1, c), lambda i: (layer, 0, 0))
    return pl.pallas_call(
        kern,
        out_shape=jax.ShapeDtypeStruct((m, c), BF16),
        grid=(n_tiles,),
        in_specs=[
            pl.BlockSpec((t, c), lambda i: (i, a_col)),
            pl.BlockSpec((t, c), lambda i: (i, g_col)),
            pl.BlockSpec((CONV_HALO, c), lambda i: (prev(i), a_col)),
            pl.BlockSpec((CONV_HALO, c), lambda i: (prev(i), g_col)),
            pl.BlockSpec((CONV_HALO, c), lambda i: (nxt(i), a_col)),
            pl.BlockSpec((CONV_HALO, c), lambda i: (nxt(i), g_col)),
            pl.BlockSpec((None, CONV_K, c), lambda i: (layer, 0, 0)),
            vec(), vec(), vec(),
        ],
        out_specs=pl.BlockSpec((t, c), lambda i: (i, 0)),
        scratch_shapes=[pltpu.VMEM((t + 2 * CONV_HALO, c), F32), pltpu.VMEM((t, c), F32)],
        compiler_params=_cparams(("parallel",)),
        name="conv_module",
    )(proj, proj, proj, proj, proj, proj, w_dw, b_dw, ln_g, ln_b)


def _merge_kernel(oa_ref, hc_ref, wa_ref, wc_ref, ga_ref, gc_ref, o_ref):
    ya = jnp.dot(oa_ref[...], wa_ref[...], preferred_element_type=F32)
    yc = jnp.dot(hc_ref[...], wc_ref[...], preferred_element_type=F32)
    y = jax.nn.sigmoid(ga_ref[...].astype(F32)) * ya + jax.nn.sigmoid(gc_ref[...].astype(F32)) * yc
    o_ref[...] = y.astype(o_ref.dtype)


def _merge(o_attn, h_conv, wa, wc, proj, layer, *, ga_col0, gc_col0, tm=512, tn=1024):
    m, k = o_attn.shape
    n = wa.shape[-1]
    return pl.pallas_call(
        _merge_kernel,
        out_shape=jax.ShapeDtypeStruct((m, n), BF16),
        grid=(n // tn, m // tm),
        in_specs=[
            pl.BlockSpec((tm, k), lambda j, i: (i, 0)),
            pl.BlockSpec((tm, k), lambda j, i: (i, 0)),
            pl.BlockSpec((None, k, tn), lambda j, i: (layer, 0, j)),
            pl.BlockSpec((None, k, tn), lambda j, i: (layer, 0, j)),
            pl.BlockSpec((tm, tn), lambda j, i: (i, ga_col0 // tn + j)),
            pl.BlockSpec((tm, tn), lambda j, i: (i, gc_col0 // tn + j)),
        ],
        out_specs=pl.BlockSpec((tm, tn), lambda j, i: (i, j)),
        compiler_params=_cparams(("parallel", "parallel")),
        name="merge",
    )(o_attn, h_conv, wa, wc, proj, proj)


def _post_residual(z_chunk, x_ref, g_ref, mod_ref, grp, gate_col, o_ref, rows=128):
    d = x_ref.shape[-1]
    gate = _mod_row(mod_ref, grp, gate_col, d)
    g = g_ref[...]

    def body(r0):
        o_ref[pl.ds(r0, rows), :] = x_ref[pl.ds(r0, rows), :] + gate * _rms(z_chunk(r0), g)

    _row_chunks(x_ref.shape[0], rows, body)


def _matmul_res_kernel(a_ref, w_ref, x_ref, g_ref, mod_ref, o_ref, acc0_ref, acc1_ref, *,
                       tiles_per_batch, n_batch, gate_col, nk, n_tiles):
    i = pl.program_id(0)
    kk = pl.program_id(1)
    tm, d = x_ref.shape
    rows = tm // nk
    gate = _mod_row(mod_ref, jnp.minimum(jnp.maximum(i - 1, 0) // tiles_per_batch, n_batch), gate_col, d)

    @pl.when(jnp.logical_and(i == 0, kk == 0))
    def _():
        acc1_ref[...] = jnp.zeros_like(acc1_ref)

    for parity, (acc_w, acc_r) in enumerate(((acc0_ref, acc1_ref), (acc1_ref, acc0_ref))):
        for k_step in range(nk):
            @pl.when(jnp.logical_and(i % 2 == parity, kk == k_step))
            def _(acc_w=acc_w, acc_r=acc_r, k_step=k_step):
                part = jnp.dot(a_ref[...], w_ref[...], preferred_element_type=F32)
                if k_step == 0:
                    acc_w[...] = part
                else:
                    acc_w[...] += part
                sub = min(rows, 128)
                for r0 in range(k_step * rows, (k_step + 1) * rows, sub):
                    z = acc_r[r0:r0 + sub, :]
                    o_ref[r0:r0 + sub, :] = x_ref[r0:r0 + sub, :] + gate * _rms(z, g_ref[...])


def _matmul_residual(a, w, xs, g, mod, layer, *, gate_col, n_batch, seq, m_rows, w_index, tm=512, tk=None):
    k = a.shape[1]
    d = xs.shape[1]
    tk = k if tk is None else tk
    nk = k // tk
    n_tiles = m_rows // tm
    kern = functools.partial(_matmul_res_kernel, tiles_per_batch=seq // tm, n_batch=n_batch,
                             gate_col=gate_col, nk=nk, n_tiles=n_tiles)
    cur = lambda i: jnp.minimum(i, n_tiles - 1)
    prev = lambda i: jnp.maximum(i - 1, 0)
    return pl.pallas_call(
        kern,
        out_shape=jax.ShapeDtypeStruct(xs.shape, F32),
        grid=(n_tiles + 1, nk),
        in_specs=[
            pl.BlockSpec((tm, tk), lambda i, kk: (cur(i), kk)),
            pl.BlockSpec((None, tk, d), lambda i, kk: (w_index, kk, 0)),
            pl.BlockSpec((tm, d), lambda i, kk: (prev(i), 0)),
            pl.BlockSpec((None, 1, d), lambda i, kk: (layer, 0, 0)),
            pl.BlockSpec((None, MOD_ROWS, N_MOD * d), lambda i, kk: (layer, 0, 0)),
        ],
        out_specs=pl.BlockSpec((tm, d), lambda i, kk: (prev(i), 0)),
        scratch_shapes=[pltpu.VMEM((tm, d), F32), pltpu.VMEM((tm, d), F32)],
        input_output_aliases={2: 0},
        compiler_params=_cparams(("arbitrary", "arbitrary")),
        name="matmul_residual",
    )(a, w, xs, g, mod)


def _ffn_in_kernel(x_ref, g_ref, mod_ref, wg_ref, wu_ref, o_ref, h_ref, *, tiles_per_batch, n_batch):
    i = pl.program_id(0)

    @pl.when(pl.program_id(1) == 0)
    def _():
        grp = jnp.minimum(i // tiles_per_batch, n_batch)
        _prenorm_into(h_ref, x_ref, g_ref, mod_ref, grp, 3, 4)

    h = h_ref[...]
    a = jnp.dot(h, wg_ref[...], preferred_element_type=F32)
    u = jnp.dot(h, wu_ref[...], preferred_element_type=F32)
    o_ref[...] = (_silu(a) * u).astype(o_ref.dtype)


def _ffn_in(xs, g, mod, wg, wu, layer, j_dense, *, n_batch, seq, m_rows, tm=1024, tn=512):
    d = xs.shape[1]
    f = wg.shape[-1]
    kern = functools.partial(_ffn_in_kernel, tiles_per_batch=seq // tm, n_batch=n_batch)
    return pl.pallas_call(
        kern,
        out_shape=jax.ShapeDtypeStruct((m_rows, f), BF16),
        grid=(m_rows // tm, f // tn),
        in_specs=[
            pl.BlockSpec((tm, d), lambda i, j: (i, 0)),
            pl.BlockSpec((None, 1, d), lambda i, j: (layer, 0, 0)),
            pl.BlockSpec((None, MOD_ROWS, N_MOD * d), lambda i, j: (layer, 0, 0)),
            pl.BlockSpec((None, d, tn), lambda i, j: (j_dense, 0, j)),
            pl.BlockSpec((None, d, tn), lambda i, j: (j_dense, 0, j)),
        ],
        out_specs=pl.BlockSpec((tm, tn), lambda i, j: (i, j)),
        scratch_shapes=[pltpu.VMEM((tm, d), BF16)],
        compiler_params=_cparams(("parallel", "arbitrary")),
        name="ffn_in",
    )(xs, g, mod, wg, wu)


def _split_bf16(a):
    hi = a.astype(BF16)
    return hi, (a - hi.astype(F32)).astype(BF16)


def _router_kernel(x_ref, g_ref, mod_ref, wh_ref, wl_ref, h_ref, top_ref, *, tiles_per_batch, n_batch):
    i = pl.program_id(0)
    grp = jnp.minimum(i // tiles_per_batch, n_batch)
    d = x_ref.shape[-1]
    shift = _mod_row(mod_ref, grp, 3, d)
    scale1 = 1.0 + _mod_row(mod_ref, grp, 4, d)
    g = g_ref[...]
    rows = 256

    def body(r0):
        h = _prenorm_chunk(x_ref[pl.ds(r0, rows), :], g, shift, scale1)
        h_ref[pl.ds(r0, rows), :] = h.astype(h_ref.dtype)
        h_hi, h_lo = _split_bf16(h)
        dot = functools.partial(jnp.dot, preferred_element_type=F32)
        logits = dot(h_hi, wh_ref[...]) + (dot(h_hi, wl_ref[...]) + dot(h_lo, wh_ref[...]))
        lane =lax.broadcasted_iota(jnp.int32, logits.shape, 1)
        big = jnp.int32(logits.shape[-1])
        logits = jnp.where(lane < N_EXPERTS, logits, -jnp.inf)
        m1 = jnp.max(logits, axis=-1, keepdims=True)
        i1 = jnp.min(jnp.where(logits == m1, lane, big), axis=-1, keepdims=True)
        rest = jnp.where(lane == i1, -jnp.inf, logits)
        m2 = jnp.max(rest, axis=-1, keepdims=True)
        i2 = jnp.min(jnp.where(rest == m2, lane, big), axis=-1, keepdims=True)
        e2 = jnp.exp(m2 - m1)
        w1 = 1.0 / (1.0 + e2)
        w2 = e2 / (1.0 + e2)
        packed = jnp.where(lane == 0, i1.astype(F32), jnp.where(lane == 1, i2.astype(F32),
                           jnp.where(lane == 2, w1, jnp.where(lane == 3, w2, 0.0))))
        top_ref[:, pl.ds(r0, rows)] = jnp.transpose(packed)[:ROUTE_ROWS, :]

    _row_chunks(x_ref.shape[0], rows, body)


def _router(xs, g, mod, w_hi, w_lo, layer, j_moe, *, n_batch, seq, m_rows, tm=512):
    d = xs.shape[1]
    lanes = w_hi.shape[-1]
    kern = functools.partial(_router_kernel, tiles_per_batch=seq // tm, n_batch=n_batch)
    return pl.pallas_call(
        kern,
        out_shape=(jax.ShapeDtypeStruct((m_rows, d), BF16),
                   jax.ShapeDtypeStruct((ROUTE_ROWS, m_rows), F32)),
        grid=(m_rows // tm,),
        in_specs=[
            pl.BlockSpec((tm, d), lambda i: (i, 0)),
            pl.BlockSpec((None, 1, d), lambda i: (layer, 0, 0)),
            pl.BlockSpec((None, MOD_ROWS, N_MOD * d), lambda i: (layer, 0, 0)),
            pl.BlockSpec((None, d, lanes), lambda i: (j_moe, 0, 0)),
            pl.BlockSpec((None, d, lanes), lambda i: (j_moe, 0, 0)),
        ],
        out_specs=(pl.BlockSpec((tm, d), lambda i: (i, 0)),
                   pl.BlockSpec((ROUTE_ROWS, tm), lambda i: (0, i))),
        compiler_params=_cparams(("parallel",)),
        name="router",
    )(xs, g, mod, w_hi, w_lo)


def _expert_block_is_new(te_ref, t):
    return jnp.logical_or(t == 0, te_ref[t] != te_ref[jnp.maximum(t - 1, 0)])


def _tile_cases(rows_ref, t, tm, full, half, empty):
    rows = rows_ref[t]
    pl.when(rows > tm // 2)(full)
    pl.when(jnp.logical_and(rows > 0, rows <= tm // 2))(half)
    pl.when(rows == 0)(empty)


def _moe_up_kernel(te_ref, rows_ref, x_ref, wg_ref, wu_ref, o_ref, wgb_ref, wub_ref):
    t = pl.program_id(1)
    tm = x_ref.shape[0]

    @pl.when(_expert_block_is_new(te_ref, t))
    def _():
        wgb_ref[...] = wg_ref[...].astype(BF16)
        wub_ref[...] = wu_ref[...].astype(BF16)

    def compute(n_rows):
        x = x_ref[:n_rows, :]
        a = jnp.dot(x, wgb_ref[...], preferred_element_type=F32)
        u = jnp.dot(x, wub_ref[...], preferred_element_type=F32)
        o_ref[:n_rows, :] = (_silu(a) * u).astype(o_ref.dtype)

    def half():
        compute(tm // 2)
        o_ref[tm // 2:, :] = jnp.zeros((tm - tm // 2, o_ref.shape[1]), o_ref.dtype)

    def empty():
        o_ref[...] = jnp.zeros_like(o_ref)

    _tile_cases(rows_ref, t, tm, functools.partial(compute, tm), half, empty)


def _moe_up(tile_expert, tile_rows, x_sorted, wg, wu, j_moe, *, tn=512):
    r, d = x_sorted.shape
    f = wg.shape[-1]
    tm = MOE_TM
    grid_spec = pltpu.PrefetchScalarGridSpec(
        num_scalar_prefetch=2,
        grid=(f // tn, r // tm),
        in_specs=[
            pl.BlockSpec((tm, d), lambda j, t, te, nu: (t, 0)),
            pl.BlockSpec((None, None, d, tn), lambda j, t, te, nu: (j_moe, te[t], 0, j)),
            pl.BlockSpec((None, None, d, tn), lambda j, t, te, nu: (j_moe, te[t], 0, j)),
        ],
        out_specs=pl.BlockSpec((tm, tn), lambda j, t, te, nu: (t, j)),
        scratch_shapes=[pltpu.VMEM((d, tn), BF16), pltpu.VMEM((d, tn), BF16)],
    )
    return pl.pallas_call(
        _moe_up_kernel,
        out_shape=jax.ShapeDtypeStruct((r, f), BF16),
        grid_spec=grid_spec,
        compiler_params=_cparams(("arbitrary", "arbitrary")),
        name="moe_up",
    )(tile_expert, tile_rows, x_sorted, wg, wu)


def _moe_down_kernel(te_ref, rows_ref, a_ref, wd_ref, rw_ref, o_ref, wdb_ref):
    t = pl.program_id(1)
    tm = a_ref.shape[0]

    @pl.when(_expert_block_is_new(te_ref, t))
    def _():
        wdb_ref[...] = wd_ref[...].astype(BF16)

    def compute(n_rows):
        y = jnp.dot(a_ref[:n_rows, :], wdb_ref[...], preferred_element_type=F32)
        o_ref[:n_rows, :] = (y * rw_ref[:n_rows, :]).astype(o_ref.dtype)

    def half():
        compute(tm // 2)
        o_ref[tm // 2:, :] = jnp.zeros((tm - tm // 2, o_ref.shape[1]), o_ref.dtype)

    def empty():
        o_ref[...] = jnp.zeros_like(o_ref)

    _tile_cases(rows_ref, t, tm, functools.partial(compute, tm), half, empty)


def _moe_down(tile_expert, tile_rows, act, wd, row_w, j_moe, *, tn=512):
    r, f = act.shape
    d = wd.shape[-1]
    tm = MOE_TM
    grid_spec = pltpu.PrefetchScalarGridSpec(
        num_scalar_prefetch=2,
        grid=(d // tn, r // tm),
        in_specs=[
            pl.BlockSpec((tm, f), lambda j, t, te, nu: (t, 0)),
            pl.BlockSpec((None, None, f, tn), lambda j, t, te, nu: (j_moe, te[t], 0, j)),
            pl.BlockSpec((tm, 1), lambda j, t, te, nu: (t, 0)),
        ],
        out_specs=pl.BlockSpec((tm, tn), lambda j, t, te, nu: (t, j)),
        scratch_shapes=[pltpu.VMEM((f, tn), BF16)],
    )
    return pl.pallas_call(
        _moe_down_kernel,
        out_shape=jax.ShapeDtypeStruct((r, d), BF16),
        grid_spec=grid_spec,
        compiler_params=_cparams(("arbitrary", "arbitrary")),
        name="moe_down",
    )(tile_expert, tile_rows, act, wd, row_w)


def _norm_res_kernel(y0_ref, y1_ref, x_ref, g_ref, mod_ref, o_ref, *, tiles_per_batch, n_batch, gate_col):
    i = pl.program_id(0)
    grp = jnp.minimum(i // tiles_per_batch, n_batch)
    rows = 128
    _post_residual(lambda r0: y0_ref[pl.ds(r0, rows), :].astype(F32) + y1_ref[pl.ds(r0, rows), :].astype(F32),
                   x_ref, g_ref, mod_ref, grp, gate_col, o_ref, rows)


def _norm_residual(y0, y1, xs, g, mod, layer, *, gate_col, n_batch, seq, m_rows, in_place, tm=512):
    d = xs.shape[1]
    kern = functools.partial(_norm_res_kernel, tiles_per_batch=seq // tm, n_batch=n_batch, gate_col=gate_col)
    row = lambda: pl.BlockSpec((tm, d), lambda i: (i, 0))
    return pl.pallas_call(
        kern,
        out_shape=jax.ShapeDtypeStruct(xs.shape if in_place else (m_rows, d), F32),
        grid=(m_rows // tm,),
        in_specs=[row(), row(), row(),
                  pl.BlockSpec((None, 1, d), lambda i: (layer, 0, 0)),
                  pl.BlockSpec((None, MOD_ROWS, N_MOD * d), lambda i: (layer, 0, 0))],
        out_specs=row(),
        input_output_aliases={2: 0} if in_place else {},
        compiler_params=_cparams(("parallel",)),
        name="norm_residual",
    )(y0, y1, xs, g, mod)


RANK_CHUNK = 512


def _rank_kernel(e_ref, tri_ref, rank_ref, cnt_ref, carry_ref):
    @pl.when(pl.program_id(0) == 0)
    def _():
        carry_ref[...] = jnp.zeros_like(carry_ref)

    e = e_ref[...]
    expert = lax.broadcasted_iota(jnp.int32, (N_EXPERTS, e.shape[1]), 0)
    onehot = (expert == e).astype(F32)
    local = jnp.dot(onehot.astype(BF16), tri_ref[...], preferred_element_type=F32)
    carry = carry_ref[...]
    csum = local + carry[:, :1]
    rank_ref[...] = (jnp.sum(csum * onehot, axis=0, keepdims=True) - 1.0).astype(jnp.int32)
    carry = carry + jnp.sum(onehot, axis=1, keepdims=True)
    carry_ref[...] = carry
    cnt_ref[...] = carry.astype(jnp.int32)


def _pair_ranks(flat_e):
    n2 = flat_e.shape[0]
    c = RANK_CHUNK
    tri = jnp.asarray(np.triu(np.ones((c, c), np.float32)), BF16)
    rank, cnt = pl.pallas_call(
        _rank_kernel,
        out_shape=(jax.ShapeDtypeStruct((1, n2), jnp.int32), jax.ShapeDtypeStruct((N_EXPERTS, 128), jnp.int32)),
        grid=(n2 // c,),
        in_specs=[pl.BlockSpec((1, c), lambda s: (0, s)), pl.BlockSpec((c, c), lambda s: (0, 0))],
        out_specs=(pl.BlockSpec((1, c), lambda s: (0, s)), pl.BlockSpec((N_EXPERTS, 128), lambda s: (0, 0))),
        scratch_shapes=[pltpu.VMEM((N_EXPERTS, 128), F32)],
        compiler_params=_cparams(("arbitrary",)),
        name="pair_ranks",
    )(flat_e.reshape(1, n2), tri)
    return rank.reshape(n2), cnt[:, 0]


def _route_tables(top, tm):
    n = top.shape[1]
    flat_e = top[0:2].astype(jnp.int32).reshape(-1)
    flat_w = top[2:4].reshape(-1)
    rank, counts = _pair_ranks(flat_e)
    padded = ((counts + tm - 1) // tm) * tm
    ends = jnp.cumsum(padded)
    starts = ends - padded
    pos = starts[flat_e] + rank
    r_pad = 2 * n + N_EXPERTS * tm
    n_tiles = r_pad // tm
    row_pair = jnp.full((r_pad,), 2 * n, jnp.int32).at[pos].set(jnp.arange(2 * n, dtype=jnp.int32))
    is_real = row_pair < 2 * n
    spread = jnp.arange(r_pad, dtype=jnp.int32) % (2 * n)
    row_pair = jnp.where(is_real, row_pair, spread)
    row_token = row_pair % n
    row_w = jnp.where(is_real, flat_w[row_pair], 0.0)
    n_used = (ends[-1] // tm).astype(jnp.int32)
    tile_start = jnp.arange(n_tiles, dtype=jnp.int32) * tm
    tile_expert = jnp.sum((tile_start[:, None] >= ends[None, :]).astype(jnp.int32), axis=1)
    tile_expert = jnp.minimum(tile_expert, N_EXPERTS - 1)
    tile_rows = jnp.clip(counts[tile_expert] - (tile_start - starts[tile_expert]), 0, tm)
    tile_rows = jnp.where(jnp.arange(n_tiles) < n_used, tile_rows, 0)
    last = tile_expert[jnp.maximum(n_used - 1, 0)]
    tile_expert = jnp.where(jnp.arange(n_tiles) < n_used, tile_expert, last)
    return (pos.reshape(2, n), row_token, row_w, tile_expert.astype(jnp.int32), tile_rows.astype(jnp.int32))


def kernel(x, c, ctx, c_ctx, w_mod, b_mod, g_mix_pre, g_mix_post, g_ffn_pre, g_ffn_post, w_in, rpb,
           w_attn_out, conv_dw, conv_db, conv_ln_g, conv_ln_b, w_conv_out, w_out, w_ff_gate, w_ff_up,
           w_ff_down, w_router, w_exp_gate, w_exp_up, w_exp_down):
    n_batch, seq, d = x.shape
    ctx_len = ctx.shape[1]
    depth = w_mod.shape[0]
    n_lat = n_batch * seq
    m_all = n_lat + n_batch * ctx_len
    assert seq == GRID_W * GRID_W and d == N_HEADS * HEAD_DIM and n_batch + 1 <= MOD_ROWS
    geo = dict(n_batch=n_batch, seq=seq)

    w_in_b = w_in.astype(BF16)
    w_ao_b = w_attn_out.astype(BF16)
    w_co_b = w_conv_out.astype(BF16)
    w_o_b = w_out.astype(BF16)
    w_fg_b, w_fu_b, w_fd_b = w_ff_gate.astype(BF16), w_ff_up.astype(BF16), w_ff_down.astype(BF16)
    w_router_hi, w_router_lo = _split_bf16(jnp.pad(w_router, ((0, 0), (0, 0), (0, 128 - N_EXPERTS))))

    vec = lambda a: a.reshape(depth, 1, -1)
    g_mix_pre, g_mix_post, g_ffn_pre, g_ffn_post = map(vec, (g_mix_pre, g_mix_post, g_ffn_pre, g_ffn_post))
    conv_db, conv_ln_g, conv_ln_b = map(vec, (conv_db, conv_ln_g, conv_ln_b))

    s = jnp.zeros((MOD_ROWS, d), F32).at[:n_batch].set(_silu(c)).at[n_batch].set(_silu(c_ctx))
    mod = _mod_table(s, w_mod, b_mod)
    slabs = _attn_bias_slabs(rpb)

    xs = jnp.concatenate([x.reshape(n_lat, d), ctx.reshape(n_batch * ctx_len, d)], axis=0)
    attn_w = N_HEADS * HEAD_DIM
    conv_w = conv_dw.shape[-1]
    u_col0 = 3 * attn_w
    ga_col0 = u_col0 + 2 * conv_w
    gc_col0 = ga_col0 + d

    for layer in range(depth):
        last = layer == depth - 1
        m_rows = n_lat if last else m_all
        j = layer // 2

        proj = _in_proj(xs, g_mix_pre, mod, w_in_b, layer, **geo)
        o_attn = _attention(proj, slabs, layer, ctx_len=ctx_len, with_ctx_queries=not last, **geo)
        h_conv = _conv_module(proj, conv_dw, conv_db, conv_ln_g, conv_ln_b, layer, ctx_len=ctx_len,
                              a_col=u_col0 // conv_w, g_col=u_col0 // conv_w + 1, **geo)
        y = _merge(o_attn, h_conv, w_ao_b, w_co_b, proj, layer, ga_col0=ga_col0, gc_col0=gc_col0)
        xs = _matmul_residual(y, w_o_b, xs, g_mix_post, mod, layer, gate_col=2, m_rows=m_rows,
                              w_index=layer, **geo)

        if layer % 2 == 0:
            act = _ffn_in(xs, g_ffn_pre, mod, w_fg_b, w_fu_b, layer, j, m_rows=m_rows, **geo)
            xs = _matmul_residual(act, w_fd_b, xs, g_ffn_post, mod, layer, gate_col=5, m_rows=m_rows,
                                  w_index=j, tk=2816, **geo)
        else:
            h, top = _router(xs, g_ffn_pre, mod, w_router_hi, w_router_lo, layer, j, m_rows=m_rows, **geo)
            pos, row_token, row_w, tile_expert, tile_rows = _route_tables(top, MOE_TM)
            x_sorted = jnp.take(h, row_token, axis=0)
            act = _moe_up(tile_expert, tile_rows, x_sorted, w_exp_gate, w_exp_up, j)
            y_sorted = _moe_down(tile_expert, tile_rows, act, w_exp_down, row_w[:, None], j)
            y0 = jnp.take(y_sorted, pos[0], axis=0)
            y1 = jnp.take(y_sorted, pos[1], axis=0)
            xs = _norm_residual(y0, y1, xs, g_ffn_post, mod, layer, gate_col=5, m_rows=m_rows,
                                in_place=not last, **geo)

    return xs[:n_lat].reshape(n_batch, seq, d)
```

```python
import functools

import numpy as np
import jax
import jax.numpy as jnp
from jax import lax
from jax.experimental import pallas as pl
from jax.experimental.pallas import tpu as pltpu

F32 = jnp.float32
BF16 = jnp.bfloat16

EPS = 1e-6
GRID_W = 64
N_HEADS = 16
HEAD_DIM = 128
CONV_K = 31
WIN_H = 8
WIN_W = 16
N_MOD = 6
N_EXPERTS = 8
MOD_ROWS = 8
ROUTE_ROWS = 8

ATT_QROWS = 4
ATT_KROWS = 12
ATT_HPB = 4
CONV_T = 256
CONV_HALO = 16
MOE_TM = 1024

VMEM_LIMIT = 56 * 1024 * 1024


def _cparams(sem):
    return pltpu.CompilerParams(dimension_semantics=sem, vmem_limit_bytes=VMEM_LIMIT)


def _silu(x):
    return x * jax.nn.sigmoid(x)


def _rms(x, g):
    return x * lax.rsqrt(jnp.mean(x * x, axis=-1, keepdims=True) + EPS) * g


def _mod_row(mod_ref, grp, col, d):
    return mod_ref[pl.ds(grp, 1), col * d:(col + 1) * d]


def _row_chunks(n_rows, rows, body):
    def step(r, carry):
        body(pl.multiple_of(r * rows, rows))
        return carry
    lax.fori_loop(0, n_rows // rows, step, 0)


def _prenorm_chunk(x, g, shift, scale1):
    return _rms(x, g) * scale1 + shift


def _prenorm_into(h_ref, x_ref, g_ref, mod_ref, grp, shift_col, scale_col, rows=128):
    d = x_ref.shape[-1]
    shift = _mod_row(mod_ref, grp, shift_col, d)
    scale1 = 1.0 + _mod_row(mod_ref, grp, scale_col, d)
    g = g_ref[...]

    def body(r0):
        x = x_ref[pl.ds(r0, rows), :]
        h_ref[pl.ds(r0, rows), :] = _prenorm_chunk(x, g, shift, scale1).astype(h_ref.dtype)

    _row_chunks(x_ref.shape[0], rows, body)


def _mod_kernel(s_ref, w_ref, b_ref, o_ref):
    o_ref[...] = jnp.dot(s_ref[...], w_ref[...], preferred_element_type=F32,
                         precision=lax.Precision.HIGHEST) + b_ref[...]


def _mod_table(s, w_mod, b_mod):
    depth, d, n = w_mod.shape
    tn = 1024
    return pl.pallas_call(
        _mod_kernel,
        out_shape=jax.ShapeDtypeStruct((depth, MOD_ROWS, n), F32),
        grid=(depth, n // tn),
        in_specs=[
            pl.BlockSpec((MOD_ROWS, d), lambda l, j: (0, 0)),
            pl.BlockSpec((None, d, tn), lambda l, j: (l, 0, j)),
            pl.BlockSpec((None, 1, tn), lambda l, j: (l, 0, j)),
        ],
        out_specs=pl.BlockSpec((None, MOD_ROWS, tn), lambda l, j: (l, 0, j)),
        compiler_params=_cparams(("parallel", "parallel")),
        name="mod_table",
    )(s, w_mod, b_mod.reshape(depth, 1, n))


def _in_proj_kernel(x_ref, g_ref, mod_ref, w_ref, o_ref, h_ref, *, tiles_per_batch, n_batch):
    i = pl.program_id(0)

    @pl.when(pl.program_id(1) == 0)
    def _():
        grp = jnp.minimum(i // tiles_per_batch, n_batch)
        _prenorm_into(h_ref, x_ref, g_ref, mod_ref, grp, 0, 1)

    o_ref[...] = jnp.dot(h_ref[...], w_ref[...], preferred_element_type=F32).astype(o_ref.dtype)


def _in_proj(xs, g, mod, w, layer, *, n_batch, seq, tm=1024, tn=2048):
    m, d = xs.shape
    n = w.shape[-1]
    kern = functools.partial(_in_proj_kernel, tiles_per_batch=seq // tm, n_batch=n_batch)
    return pl.pallas_call(
        kern,
        out_shape=jax.ShapeDtypeStruct((m, n), BF16),
        grid=(m // tm, n // tn),
        in_specs=[
            pl.BlockSpec((tm, d), lambda i, j: (i, 0)),
            pl.BlockSpec((None, 1, d), lambda i, j: (layer, 0, 0)),
            pl.BlockSpec((None, MOD_ROWS, N_MOD * d), lambda i, j: (layer, 0, 0)),
            pl.BlockSpec((None, d, tn), lambda i, j: (layer, 0, j)),
        ],
        out_specs=pl.BlockSpec((tm, tn), lambda i, j: (i, j)),
        scratch_shapes=[pltpu.VMEM((tm, d), BF16)],
        compiler_params=_cparams(("parallel", "arbitrary")),
        name="in_proj",
    )(xs, g, mod, w)


def _attn_tables():
    rows = GRID_W
    qi = np.arange(ATT_QROWS)[:, None]
    kr = np.arange(ATT_KROWS)[None, :]
    drs, row_oks = [], []
    for i0, start in ((0, 0), (2 * ATT_QROWS, ATT_QROWS), (rows - ATT_QROWS, rows - ATT_KROWS)):
        i = i0 + qi
        r = start + kr
        r0 = np.clip(i - WIN_H // 2, 0, rows - WIN_H)
        row_oks.append((r >= r0) & (r < r0 + WIN_H))
        drs.append(np.clip(r - i + (WIN_H - 1), 0, 2 * WIN_H - 2))
    qj = np.arange(GRID_W)[:, None]
    kc = np.arange(GRID_W)[None, :]
    ws = np.clip(qj - WIN_W // 2, 0, GRID_W - WIN_W)
    col_ok = (kc >= ws) & (kc < ws + WIN_W)
    dc = np.clip(kc - qj, -(WIN_W - 1), WIN_W - 1) + (WIN_W - 1)
    return np.stack(drs).astype(np.int32), np.stack(row_oks), dc.astype(np.int32), col_ok


def _attn_pair_plan():
    dr, row_ok, _, _ = _attn_tables()
    entries, plan = [], []
    for pat in range(dr.shape[0]):
        per_q = []
        for qi in range(ATT_QROWS):
            per_p = []
            for p in range(ATT_KROWS // 2):
                halves = tuple(int(dr[pat, qi, kr]) if row_ok[pat, qi, kr] else None for kr in (2 * p, 2 * p + 1))
                if halves == (None, None):
                    per_p.append(None)
                    continue
                if halves not in entries:
                    entries.append(halves)
                per_p.append(entries.index(halves))
            per_q.append(per_p)
        plan.append(per_q)
    return entries, plan


_LOG2E = float(np.log2(np.e))


def _attn_bias_slabs(rpb):
    _, _, dc, col_ok = _attn_tables()
    entries, _ = _attn_pair_plan()
    onehot = (dc[:, :, None] == np.arange(2 * WIN_W - 1)[None, None, :]).astype(np.float32)
    t = jnp.einsum("lhdk,jck->lhdjc", rpb.astype(F32), onehot, precision=lax.Precision.HIGHEST)
    t = jnp.where(col_ok, t * _LOG2E, -jnp.inf)
    masked = jnp.full(t.shape[:2] + t.shape[3:], -jnp.inf, F32)
    half = lambda d: masked if d is None else t[:, :, d]
    return jnp.stack([jnp.concatenate([half(a), half(b)], axis=-1) for a, b in entries], axis=2)


def _attn_kernel(q_ref, k_ref, v_ref, kc_ref, vc_ref, slab_ref, o_ref, bias_ref, *,
                 n_qblocks, max_start, with_ctx_queries):
    rb = pl.program_id(2)
    nk = ATT_KROWS * GRID_W
    scale2 = HEAD_DIM ** -0.5 * _LOG2E
    dn = (((1,), (1,)), ((), ()))
    _, plan = _attn_pair_plan()

    def scores(q, k):
        return lax.dot_general(q, k, dn, preferred_element_type=F32) * scale2

    def assemble(pattern):
        masked = jnp.full((GRID_W, 2 * GRID_W), -jnp.inf, F32)
        for hh in range(ATT_HPB):
            for qi in range(ATT_QROWS):
                for p, e in enumerate(plan[pattern][qi]):
                    blk = masked if e is None else slab_ref[hh, e]
                    bias_ref[hh, qi * GRID_W:(qi + 1) * GRID_W, 2 * p * GRID_W:2 * (p + 1) * GRID_W] = blk

    for pattern, first_rb in ((0, 0), (1, 1), (2, n_qblocks - 1)):
        pl.when(rb == first_rb)(functools.partial(assemble, pattern))

    @pl.when(rb < n_qblocks)
    def _():
        start = jnp.clip(rb * ATT_QROWS - WIN_H // 2, 0, max_start)
        tok0 = pl.multiple_of(start * GRID_W, GRID_W)
        heads = [slice(hh * HEAD_DIM, (hh + 1) * HEAD_DIM) for hh in range(ATT_HPB)]

        s_all = [(scores(q_ref[:, hs], k_ref[pl.ds(tok0, nk), hs]) + bias_ref[hh], scores(q_ref[:, hs], kc_ref[:, hs]))
                 for hh, hs in enumerate(heads)]
        for hs, (s_loc, s_ctx) in zip(heads, s_all):
            m = jnp.maximum(jnp.max(s_loc, axis=-1, keepdims=True), jnp.max(s_ctx, axis=-1, keepdims=True))
            p_loc = jnp.exp2(s_loc - m)
            p_ctx = jnp.exp2(s_ctx - m)
            l = jnp.sum(p_loc, axis=-1, keepdims=True) + jnp.sum(p_ctx, axis=-1, keepdims=True)
            o = (jnp.dot(p_loc.astype(BF16), v_ref[pl.ds(tok0, nk), hs], preferred_element_type=F32)
                 + jnp.dot(p_ctx.astype(BF16), vc_ref[:, hs], preferred_element_type=F32))
            o_ref[:, hs] = (o / l).astype(o_ref.dtype)

    if with_ctx_queries:
        @pl.when(rb == n_qblocks)
        def _():
            for hh in range(ATT_HPB):
                hs = slice(hh * HEAD_DIM, (hh + 1) * HEAD_DIM)
                s_ctx = scores(q_ref[:, hs], kc_ref[:, hs])
                p_ctx = jnp.exp2(s_ctx - jnp.max(s_ctx, axis=-1, keepdims=True))
                l = jnp.sum(p_ctx, axis=-1, keepdims=True)
                o = jnp.dot(p_ctx.astype(BF16), vc_ref[:, hs], preferred_element_type=F32)
                o_ref[:, hs] = (o / l).astype(o_ref.dtype)


def _attention(proj, slabs, layer, *, n_batch, seq, ctx_len, with_ctx_queries):
    m = proj.shape[0]
    nq = ATT_QROWS * GRID_W
    assert nq == ctx_len, "context queries reuse the latent query-block shape"
    rows = seq // GRID_W
    n_qblocks = rows // ATT_QROWS
    ctx_blk0 = (n_batch * seq) // ctx_len
    n_hg = N_HEADS // ATT_HPB
    wblk = ATT_HPB * HEAD_DIM

    def qrow(b, rb):
        return jnp.where(rb == n_qblocks, ctx_blk0 + b, b * n_qblocks + rb)

    assert n_qblocks >= 3, "first / interior / last bias patterns need distinct query blocks"
    n_entries = slabs.shape[2]
    kern = functools.partial(_attn_kernel, n_qblocks=n_qblocks, max_start=rows - ATT_KROWS,
                             with_ctx_queries=with_ctx_queries)
    return pl.pallas_call(
        kern,
        out_shape=jax.ShapeDtypeStruct((m, N_HEADS * HEAD_DIM), BF16),
        grid=(n_batch, n_hg, n_qblocks + (1 if with_ctx_queries else 0)),
        in_specs=[
            pl.BlockSpec((nq, wblk), lambda b, h, rb: (qrow(b, rb), h)),
            pl.BlockSpec((seq, wblk), lambda b, h, rb: (b, n_hg + h)),
            pl.BlockSpec((seq, wblk), lambda b, h, rb: (b, 2 * n_hg + h)),
            pl.BlockSpec((ctx_len, wblk), lambda b, h, rb: (ctx_blk0 + b, n_hg + h)),
            pl.BlockSpec((ctx_len, wblk), lambda b, h, rb: (ctx_blk0 + b, 2 * n_hg + h)),
            pl.BlockSpec((None, ATT_HPB, n_entries, GRID_W, 2 * GRID_W), lambda b, h, rb: (layer, h, 0, 0, 0)),
        ],
        out_specs=pl.BlockSpec((nq, wblk), lambda b, h, rb: (qrow(b, rb), h)),
        scratch_shapes=[pltpu.VMEM((ATT_HPB, nq, ATT_KROWS * GRID_W), F32)],
        compiler_params=_cparams(("parallel", "parallel", "arbitrary")),
        name="attention",
    )(proj, proj, proj, proj, proj, slabs)


def _conv_kernel(a_ref, g_ref, ap_ref, gp_ref, an_ref, gn_ref, w_ref, b_ref, lg_ref, lb_ref,
                 o_ref, hext_ref, y_ref, *, tiles_per_seq, n_latent_tiles):
    i = pl.program_id(0)
    t = CONV_T
    halo = CONV_HALO
    c = a_ref.shape[-1]
    is_ctx = i >= n_latent_tiles
    pos = i % tiles_per_seq
    at_start = jnp.logical_or(is_ctx, pos == 0)
    at_end = jnp.logical_or(is_ctx, pos == tiles_per_seq - 1)

    def glu(a, g):
        return a.astype(F32) * jax.nn.sigmoid(g.astype(F32))

    lanes = 128
    rchunk = 64
    off = halo - CONV_K // 2

    hext_ref[0:halo, :] = jnp.where(at_start, 0.0, glu(ap_ref[...], gp_ref[...]))
    hext_ref[halo + t:halo + t + halo, :] = jnp.where(at_end, 0.0, glu(an_ref[...], gn_ref[...]))

    def glu_body(r0):
        dst = pl.multiple_of(r0 + halo, halo)
        hext_ref[pl.ds(dst, rchunk), :] = glu(a_ref[pl.ds(r0, rchunk), :], g_ref[pl.ds(r0, rchunk), :])

    _row_chunks(t, rchunk, glu_body)

    def chunk_body(ci, carry):
        c0 = pl.multiple_of(ci * lanes, lanes)
        wts = w_ref[:, pl.ds(c0, lanes)]
        bias = b_ref[:, pl.ds(c0, lanes)]
        for r0 in range(0, t, rchunk):
            acc = bias
            for res in range(8):
                n_rows = rchunk + (8 if res else 0)
                part = None
                for base in range(0, CONV_K + off, 8):
                    k = base + res - off
                    if 0 <= k < CONV_K:
                        term = wts[k:k + 1, :] * hext_ref[r0 + base:r0 + base + n_rows, pl.ds(c0, lanes)]
                        part = term if part is None else part + term
                if part is not None:
                    acc = acc + part[res:res + rchunk, :]
            y_ref[r0:r0 + rchunk, pl.ds(c0, lanes)] = acc
        return carry

    lax.fori_loop(0, c // lanes, chunk_body, 0)

    ln_g = lg_ref[...]
    ln_b = lb_ref[...]

    def ln_body(r0):
        y = y_ref[pl.ds(r0, rchunk), :]
        mu = jnp.mean(y, axis=-1, keepdims=True)
        yc = y - mu
        var = jnp.mean(yc * yc, axis=-1, keepdims=True)
        z = yc * lax.rsqrt(var + EPS) * ln_g + ln_b
        o_ref[pl.ds(r0, rchunk), :] = _silu(z).astype(o_ref.dtype)

    _row_chunks(t, rchunk, ln_body)


def _conv_module(proj, w_dw, b_dw, ln_g, ln_b, layer, *, n_batch, seq, ctx_len, a_col, g_col):
    m = proj.shape[0]
    c = w_dw.shape[-1]
    t = CONV_T
    assert ctx_len == t and seq % t == 0
    hb = t // CONV_HALO
    last_hblk = m // CONV_HALO - 1
    n_tiles = m // t
    kern = functools.partial(_conv_kernel, tiles_per_seq=seq // t, n_latent_tiles=(n_batch * seq) // t)

    def prev(i):
        return jnp.maximum(i * hb - 1, 0)

    def nxt(i):
        return jnp.minimum(i * hb + hb, last_hblk)

    vec = lambda: pl.BlockSpec((None, 1, c), lambda i: (layer, 0, 0))
    return pl.pallas_call(
        kern,
        out_shape=jax.ShapeDtypeStruct((m, c), BF16),
        grid=(n_tiles,),
        in_specs=[
            pl.BlockSpec((t, c), lambda i: (i, a_col)),
            pl.BlockSpec((t, c), lambda i: (i, g_col)),
            pl.BlockSpec((CONV_HALO, c), lambda i: (prev(i), a_col)),
            pl.BlockSpec((CONV_HALO, c), lambda i: (prev(i), g_col)),
            pl.BlockSpec((CONV_HALO, c), lambda i: (nxt(i), a_col)),
            pl.BlockSpec((CONV_HALO, c), lambda i: (nxt(i), g_col)),
            pl.BlockSpec((None, CONV_K, c), lambda i: (layer, 0, 0)),
            vec(), vec(), vec(),
        ],
        out_specs=pl.BlockSpec((t, c), lambda i: (i, 0)),
        scratch_shapes=[pltpu.VMEM((t + 2 * CONV_HALO, c), F32), pltpu.VMEM((t, c), F32)],
        compiler_params=_cparams(("parallel",)),
        name="conv_module",
    )(proj, proj, proj, proj, proj, proj, w_dw, b_dw, ln_g, ln_b)


def _merge_kernel(oa_ref, hc_ref, wa_ref, wc_ref, ga_ref, gc_ref, o_ref):
    ya = jnp.dot(oa_ref[...], wa_ref[...], preferred_element_type=F32)
    yc = jnp.dot(hc_ref[...], wc_ref[...], preferred_element_type=F32)
    y = jax.nn.sigmoid(ga_ref[...].astype(F32)) * ya + jax.nn.sigmoid(gc_ref[...].astype(F32)) * yc
    o_ref[...] = y.astype(o_ref.dtype)


def _merge(o_attn, h_conv, wa, wc, proj, layer, *, ga_col0, gc_col0, tm=512, tn=1024):
    m, k = o_attn.shape
    n = wa.shape[-1]
    return pl.pallas_call(
        _merge_kernel,
        out_shape=jax.ShapeDtypeStruct((m, n), BF16),
        grid=(n // tn, m // tm),
        in_specs=[
            pl.BlockSpec((tm, k), lambda j, i: (i, 0)),
            pl.BlockSpec((tm, k), lambda j, i: (i, 0)),
            pl.BlockSpec((None, k, tn), lambda j, i: (layer, 0, j)),
            pl.BlockSpec((None, k, tn), lambda j, i: (layer, 0, j)),
            pl.BlockSpec((tm, tn), lambda j, i: (i, ga_col0 // tn + j)),
            pl.BlockSpec((tm, tn), lambda j, i: (i, gc_col0 // tn + j)),
        ],
        out_specs=pl.BlockSpec((tm, tn), lambda j, i: (i, j)),
        compiler_params=_cparams(("parallel", "parallel")),
        name="merge",
    )(o_attn, h_conv, wa, wc, proj, proj)


def _post_residual(z_chunk, x_ref, g_ref, mod_ref, grp, gate_col, o_ref, rows=128):
    d = x_ref.shape[-1]
    gate = _mod_row(mod_ref, grp, gate_col, d)
    g = g_ref[...]

    def body(r0):
        o_ref[pl.ds(r0, rows), :] = x_ref[pl.ds(r0, rows), :] + gate * _rms(z_chunk(r0), g)

    _row_chunks(x_ref.shape[0], rows, body)


def _matmul_res_kernel(a_ref, w_ref, x_ref, g_ref, mod_ref, o_ref, acc0_ref, acc1_ref, *,
                       tiles_per_batch, n_batch, gate_col, nk, n_tiles):
    i = pl.program_id(0)
    kk = pl.program_id(1)
    tm, d = x_ref.shape
    rows = tm // nk
    gate = _mod_row(mod_ref, jnp.minimum(jnp.maximum(i - 1, 0) // tiles_per_batch, n_batch), gate_col, d)

    @pl.when(jnp.logical_and(i == 0, kk == 0))
    def _():
        acc1_ref[...] = jnp.zeros_like(acc1_ref)

    for parity, (acc_w, acc_r) in enumerate(((acc0_ref, acc1_ref), (acc1_ref, acc0_ref))):
        for k_step in range(nk):
            @pl.when(jnp.logical_and(i % 2 == parity, kk == k_step))
            def _(acc_w=acc_w, acc_r=acc_r, k_step=k_step):
                part = jnp.dot(a_ref[...], w_ref[...], preferred_element_type=F32)
                if k_step == 0:
                    acc_w[...] = part
                else:
                    acc_w[...] += part
                sub = min(rows, 128)
                for r0 in range(k_step * rows, (k_step + 1) * rows, sub):
                    z = acc_r[r0:r0 + sub, :]
                    o_ref[r0:r0 + sub, :] = x_ref[r0:r0 + sub, :] + gate * _rms(z, g_ref[...])


def _matmul_residual(a, w, xs, g, mod, layer, *, gate_col, n_batch, seq, m_rows, w_index, tm=512, tk=None):
    k = a.shape[1]
    d = xs.shape[1]
    tk = k if tk is None else tk
    nk = k // tk
    n_tiles = m_rows // tm
    kern = functools.partial(_matmul_res_kernel, tiles_per_batch=seq // tm, n_batch=n_batch,
                             gate_col=gate_col, nk=nk, n_tiles=n_tiles)
    cur = lambda i: jnp.minimum(i, n_tiles - 1)
    prev = lambda i: jnp.maximum(i - 1, 0)
    return pl.pallas_call(
        kern,
        out_shape=jax.ShapeDtypeStruct(xs.shape, F32),
        grid=(n_tiles + 1, nk),
        in_specs=[
            pl.BlockSpec((tm, tk), lambda i, kk: (cur(i), kk)),
            pl.BlockSpec((None, tk, d), lambda i, kk: (w_index, kk, 0)),
            pl.BlockSpec((tm, d), lambda i, kk: (prev(i), 0)),
            pl.BlockSpec((None, 1, d), lambda i, kk: (layer, 0, 0)),
            pl.BlockSpec((None, MOD_ROWS, N_MOD * d), lambda i, kk: (layer, 0, 0)),
        ],
        out_specs=pl.BlockSpec((tm, d), lambda i, kk: (prev(i), 0)),
        scratch_shapes=[pltpu.VMEM((tm, d), F32), pltpu.VMEM((tm, d), F32)],
        input_output_aliases={2: 0},
        compiler_params=_cparams(("arbitrary", "arbitrary")),
        name="matmul_residual",
    )(a, w, xs, g, mod)


def _ffn_in_kernel(x_ref, g_ref, mod_ref, wg_ref, wu_ref, o_ref, h_ref, *, tiles_per_batch, n_batch):
    i = pl.program_id(0)

    @pl.when(pl.program_id(1) == 0)
    def _():
        grp = jnp.minimum(i // tiles_per_batch, n_batch)
        _prenorm_into(h_ref, x_ref, g_ref, mod_ref, grp, 3, 4)

    h = h_ref[...]
    a = jnp.dot(h, wg_ref[...], preferred_element_type=F32)
    u = jnp.dot(h, wu_ref[...], preferred_element_type=F32)
    o_ref[...] = (_silu(a) * u).astype(o_ref.dtype)


def _ffn_in(xs, g, mod, wg, wu, layer, j_dense, *, n_batch, seq, m_rows, tm=1024, tn=512):
    d = xs.shape[1]
    f = wg.shape[-1]
    kern = functools.partial(_ffn_in_kernel, tiles_per_batch=seq // tm, n_batch=n_batch)
    return pl.pallas_call(
        kern,
        out_shape=jax.ShapeDtypeStruct((m_rows, f), BF16),
        grid=(m_rows // tm, f // tn),
        in_specs=[
            pl.BlockSpec((tm, d), lambda i, j: (i, 0)),
            pl.BlockSpec((None, 1, d), lambda i, j: (layer, 0, 0)),
            pl.BlockSpec((None, MOD_ROWS, N_MOD * d), lambda i, j: (layer, 0, 0)),
            pl.BlockSpec((None, d, tn), lambda i, j: (j_dense, 0, j)),
            pl.BlockSpec((None, d, tn), lambda i, j: (j_dense, 0, j)),
        ],
        out_specs=pl.BlockSpec((tm, tn), lambda i, j: (i, j)),
        scratch_shapes=[pltpu.VMEM((tm, d), BF16)],
        compiler_params=_cparams(("parallel", "arbitrary")),
        name="ffn_in",
    )(xs, g, mod, wg, wu)


def _split_bf16(a):
    hi = a.astype(BF16)
    return hi, (a - hi.astype(F32)).astype(BF16)


def _router_kernel(x_ref, g_ref, mod_ref, wh_ref, wl_ref, h_ref, top_ref, *, tiles_per_batch, n_batch):
    i = pl.program_id(0)
    grp = jnp.minimum(i // tiles_per_batch, n_batch)
    d = x_ref.shape[-1]
    shift = _mod_row(mod_ref, grp, 3, d)
    scale1 = 1.0 + _mod_row(mod_ref, grp, 4, d)
    g = g_ref[...]
    rows = 256

    def body(r0):
        h = _prenorm_chunk(x_ref[pl.ds(r0, rows), :], g, shift, scale1)
        h_ref[pl.ds(r0, rows), :] = h.astype(h_ref.dtype)
        h_hi, h_lo = _split_bf16(h)
        dot = functools.partial(jnp.dot, preferred_element_type=F32)
        logits = dot(h_hi, wh_ref[...]) + (dot(h_hi, wl_ref[...]) + dot(h_lo, wh_ref[...]))
        lane =lax.broadcasted_iota(jnp.int32, logits.shape, 1)
        big = jnp.int32(logits.shape[-1])
        logits = jnp.where(lane < N_EXPERTS, logits, -jnp.inf)
        m1 = jnp.max(logits, axis=-1, keepdims=True)
        i1 = jnp.min(jnp.where(logits == m1, lane, big), axis=-1, keepdims=True)
        rest = jnp.where(lane == i1, -jnp.inf, logits)
        m2 = jnp.max(rest, axis=-1, keepdims=True)
        i2 = jnp.min(jnp.where(rest == m2, lane, big), axis=-1, keepdims=True)
        e2 = jnp.exp(m2 - m1)
        w1 = 1.0 / (1.0 + e2)
        w2 = e2 / (1.0 + e2)
        packed = jnp.where(lane == 0, i1.astype(F32), jnp.where(lane == 1, i2.astype(F32),
                           jnp.where(lane == 2, w1, jnp.where(lane == 3, w2, 0.0))))
        top_ref[:, pl.ds(r0, rows)] = jnp.transpose(packed)[:ROUTE_ROWS, :]

    _row_chunks(x_ref.shape[0], rows, body)


def _router(xs, g, mod, w_hi, w_lo, layer, j_moe, *, n_batch, seq, m_rows, tm=512):
    d = xs.shape[1]
    lanes = w_hi.shape[-1]
    kern = functools.partial(_router_kernel, tiles_per_batch=seq // tm, n_batch=n_batch)
    return pl.pallas_call(
        kern,
        out_shape=(jax.ShapeDtypeStruct((m_rows, d), BF16),
                   jax.ShapeDtypeStruct((ROUTE_ROWS, m_rows), F32)),
        grid=(m_rows // tm,),
        in_specs=[
            pl.BlockSpec((tm, d), lambda i: (i, 0)),
            pl.BlockSpec((None, 1, d), lambda i: (layer, 0, 0)),
            pl.BlockSpec((None, MOD_ROWS, N_MOD * d), lambda i: (layer, 0, 0)),
            pl.BlockSpec((None, d, lanes), lambda i: (j_moe, 0, 0)),
            pl.BlockSpec((None, d, lanes), lambda i: (j_moe, 0, 0)),
        ],
        out_specs=(pl.BlockSpec((tm, d), lambda i: (i, 0)),
                   pl.BlockSpec((ROUTE_ROWS, tm), lambda i: (0, i))),
        compiler_params=_cparams(("parallel",)),
        name="router",
    )(xs, g, mod, w_hi, w_lo)


def _expert_block_is_new(te_ref, t):
    return jnp.logical_or(t == 0, te_ref[t] != te_ref[jnp.maximum(t - 1, 0)])


def _tile_cases(rows_ref, t, tm, full, half, empty):
    rows = rows_ref[t]
    pl.when(rows > tm // 2)(full)
    pl.when(jnp.logical_and(rows > 0, rows <= tm // 2))(half)
    pl.when(rows == 0)(empty)


def _moe_up_kernel(te_ref, rows_ref, x_ref, wg_ref, wu_ref, o_ref, wgb_ref, wub_ref):
    t = pl.program_id(1)
    tm = x_ref.shape[0]

    @pl.when(_expert_block_is_new(te_ref, t))
    def _():
        wgb_ref[...] = wg_ref[...].astype(BF16)
        wub_ref[...] = wu_ref[...].astype(BF16)

    def compute(n_rows):
        x = x_ref[:n_rows, :]
        a = jnp.dot(x, wgb_ref[...], preferred_element_type=F32)
        u = jnp.dot(x, wub_ref[...], preferred_element_type=F32)
        o_ref[:n_rows, :] = (_silu(a) * u).astype(o_ref.dtype)

    def half():
        compute(tm // 2)
        o_ref[tm // 2:, :] = jnp.zeros((tm - tm // 2, o_ref.shape[1]), o_ref.dtype)

    def empty():
        o_ref[...] = jnp.zeros_like(o_ref)

    _tile_cases(rows_ref, t, tm, functools.partial(compute, tm), half, empty)


def _moe_up(tile_expert, tile_rows, x_sorted, wg, wu, j_moe, *, tn=512):
    r, d = x_sorted.shape
    f = wg.shape[-1]
    tm = MOE_TM
    grid_spec = pltpu.PrefetchScalarGridSpec(
        num_scalar_prefetch=2,
        grid=(f // tn, r // tm),
        in_specs=[
            pl.BlockSpec((tm, d), lambda j, t, te, nu: (t, 0)),
            pl.BlockSpec((None, None, d, tn), lambda j, t, te, nu: (j_moe, te[t], 0, j)),
            pl.BlockSpec((None, None, d, tn), lambda j, t, te, nu: (j_moe, te[t], 0, j)),
        ],
        out_specs=pl.BlockSpec((None, tm, tn), lambda j, t, te, nu: (j, t, 0)),
        scratch_shapes=[pltpu.VMEM((d, tn), BF16), pltpu.VMEM((d, tn), BF16)],
    )
    return pl.pallas_call(
        _moe_up_kernel,
        out_shape=jax.ShapeDtypeStruct((f // tn, r, tn), BF16),
        grid_spec=grid_spec,
        compiler_params=_cparams(("arbitrary", "arbitrary")),
        name="moe_up",
    )(tile_expert, tile_rows, x_sorted, wg, wu)


def _moe_down_kernel(te_ref, rows_ref, a_ref, wd_ref, rw_ref, o_ref, wdb_ref, *, tiles_per_up_tile):
    t = pl.program_id(1)
    n_chunks, tm, tk = a_ref.shape
    up_tile = t // tiles_per_up_tile
    rows = jnp.clip(rows_ref[up_tile] - (t % tiles_per_up_tile) * tm, 0, tm)

    @pl.when(_expert_block_is_new(te_ref, up_tile) & (t % tiles_per_up_tile == 0))
    def _():
        wdb_ref[...] = wd_ref[...].astype(BF16)

    @pl.when(rows > 0)
    def _():
        y = jnp.dot(a_ref[0], wdb_ref[0:tk, :], preferred_element_type=F32)
        for c in range(1, n_chunks):
            y = y + jnp.dot(a_ref[c], wdb_ref[c * tk:(c + 1) * tk, :], preferred_element_type=F32)
        o_ref[...] = (y * rw_ref[...]).astype(o_ref.dtype)

    @pl.when(rows == 0)
    def _():
        o_ref[...] = jnp.zeros_like(o_ref)


def _moe_down(tile_expert, tile_rows, act, wd, row_w, j_moe, *, tm=512, tn=1024):
    n_chunks, r, tk = act.shape
    f = n_chunks * tk
    d = wd.shape[-1]
    per_up = MOE_TM // tm
    kern = functools.partial(_moe_down_kernel, tiles_per_up_tile=per_up)
    grid_spec = pltpu.PrefetchScalarGridSpec(
        num_scalar_prefetch=2,
        grid=(d // tn, r // tm),
        in_specs=[
            pl.BlockSpec((n_chunks, tm, tk), lambda j, t, te, nu: (0, t, 0)),
            pl.BlockSpec((None, None, f, tn), lambda j, t, te, nu: (j_moe, te[t // per_up], 0, j)),
            pl.BlockSpec((tm, 1), lambda j, t, te, nu: (t, 0)),
        ],
        out_specs=pl.BlockSpec((tm, tn), lambda j, t, te, nu: (t, j)),
        scratch_shapes=[pltpu.VMEM((f, tn), BF16)],
    )
    return pl.pallas_call(
        kern,
        out_shape=jax.ShapeDtypeStruct((r, d), BF16),
        grid_spec=grid_spec,
        compiler_params=_cparams(("arbitrary", "arbitrary")),
        name="moe_down",
    )(tile_expert, tile_rows, act, wd, row_w)


def _norm_res_kernel(y0_ref, y1_ref, x_ref, g_ref, mod_ref, o_ref, *, tiles_per_batch, n_batch, gate_col):
    i = pl.program_id(0)
    grp = jnp.minimum(i // tiles_per_batch, n_batch)
    rows = 128
    _post_residual(lambda r0: y0_ref[pl.ds(r0, rows), :].astype(F32) + y1_ref[pl.ds(r0, rows), :].astype(F32),
                   x_ref, g_ref, mod_ref, grp, gate_col, o_ref, rows)


def _norm_residual(y0, y1, xs, g, mod, layer, *, gate_col, n_batch, seq, m_rows, in_place, tm=512):
    d = xs.shape[1]
    kern = functools.partial(_norm_res_kernel, tiles_per_batch=seq // tm, n_batch=n_batch, gate_col=gate_col)
    row = lambda: pl.BlockSpec((tm, d), lambda i: (i, 0))
    return pl.pallas_call(
        kern,
        out_shape=jax.ShapeDtypeStruct(xs.shape if in_place else (m_rows, d), F32),
        grid=(m_rows // tm,),
        in_specs=[row(), row(), row(),
                  pl.BlockSpec((None, 1, d), lambda i: (layer, 0, 0)),
                  pl.BlockSpec((None, MOD_ROWS, N_MOD * d), lambda i: (layer, 0, 0))],
        out_specs=row(),
        input_output_aliases={2: 0} if in_place else {},
        compiler_params=_cparams(("parallel",)),
        name="norm_residual",
    )(y0, y1, xs, g, mod)


RANK_CHUNK = 512


def _rank_kernel(e_ref, tri_ref, rank_ref, cnt_ref, carry_ref):
    @pl.when(pl.program_id(0) == 0)
    def _():
        carry_ref[...] = jnp.zeros_like(carry_ref)

    e = e_ref[...]
    expert = lax.broadcasted_iota(jnp.int32, (N_EXPERTS, e.shape[1]), 0)
    onehot = (expert == e).astype(F32)
    local = jnp.dot(onehot.astype(BF16), tri_ref[...], preferred_element_type=F32)
    carry = carry_ref[...]
    csum = local + carry[:, :1]
    rank_ref[...] = (jnp.sum(csum * onehot, axis=0, keepdims=True) - 1.0).astype(jnp.int32)
    carry = carry + jnp.sum(onehot, axis=1, keepdims=True)
    carry_ref[...] = carry
    cnt_ref[...] = carry.astype(jnp.int32)


def _pair_ranks(flat_e):
    n2 = flat_e.shape[0]
    c = RANK_CHUNK
    tri = jnp.asarray(np.triu(np.ones((c, c), np.float32)), BF16)
    rank, cnt = pl.pallas_call(
        _rank_kernel,
        out_shape=(jax.ShapeDtypeStruct((1, n2), jnp.int32), jax.ShapeDtypeStruct((N_EXPERTS, 128), jnp.int32)),
        grid=(n2 // c,),
        in_specs=[pl.BlockSpec((1, c), lambda s: (0, s)), pl.BlockSpec((c, c), lambda s: (0, 0))],
        out_specs=(pl.BlockSpec((1, c), lambda s: (0, s)), pl.BlockSpec((N_EXPERTS, 128), lambda s: (0, 0))),
        scratch_shapes=[pltpu.VMEM((N_EXPERTS, 128), F32)],
        compiler_params=_cparams(("arbitrary",)),
        name="pair_ranks",
    )(flat_e.reshape(1, n2), tri)
    return rank.reshape(n2), cnt[:, 0]


def _route_tables(top, tm):
    n = top.shape[1]
    flat_e = top[0:2].astype(jnp.int32).reshape(-1)
    flat_w = top[2:4].reshape(-1)
    rank, counts = _pair_ranks(flat_e)
    padded = ((counts + tm - 1) // tm) * tm
    ends = jnp.cumsum(padded)
    starts = ends - padded
    pos = starts[flat_e] + rank
    r_pad = 2 * n + N_EXPERTS * tm
    n_tiles = r_pad // tm
    row_pair = jnp.full((r_pad,), 2 * n, jnp.int32).at[pos].set(jnp.arange(2 * n, dtype=jnp.int32))
    is_real = row_pair < 2 * n
    spread = jnp.arange(r_pad, dtype=jnp.int32) % (2 * n)
    row_pair = jnp.where(is_real, row_pair, spread)
    row_token = row_pair % n
    row_w = jnp.where(is_real, flat_w[row_pair], 0.0)
    n_used = (ends[-1] // tm).astype(jnp.int32)
    tile_start = jnp.arange(n_tiles, dtype=jnp.int32) * tm
    tile_expert = jnp.sum((tile_start[:, None] >= ends[None, :]).astype(jnp.int32), axis=1)
    tile_expert = jnp.minimum(tile_expert, N_EXPERTS - 1)
    tile_rows = jnp.clip(counts[tile_expert] - (tile_start - starts[tile_expert]), 0, tm)
    tile_rows = jnp.where(jnp.arange(n_tiles) < n_used, tile_rows, 0)
    last = tile_expert[jnp.maximum(n_used - 1, 0)]
    tile_expert = jnp.where(jnp.arange(n_tiles) < n_used, tile_expert, last)
    return (pos.reshape(2, n), row_token, row_w, tile_expert.astype(jnp.int32), tile_rows.astype(jnp.int32))


def kernel(x, c, ctx, c_ctx, w_mod, b_mod, g_mix_pre, g_mix_post, g_ffn_pre, g_ffn_post, w_in, rpb,
           w_attn_out, conv_dw, conv_db, conv_ln_g, conv_ln_b, w_conv_out, w_out, w_ff_gate, w_ff_up,
           w_ff_down, w_router, w_exp_gate, w_exp_up, w_exp_down):
    n_batch, seq, d = x.shape
    ctx_len = ctx.shape[1]
    depth = w_mod.shape[0]
    n_lat = n_batch * seq
    m_all = n_lat + n_batch * ctx_len
    assert seq == GRID_W * GRID_W and d == N_HEADS * HEAD_DIM and n_batch + 1 <= MOD_ROWS
    geo = dict(n_batch=n_batch, seq=seq)

    w_in_b = w_in.astype(BF16)
    w_ao_b = w_attn_out.astype(BF16)
    w_co_b = w_conv_out.astype(BF16)
    w_o_b = w_out.astype(BF16)
    w_fg_b, w_fu_b, w_fd_b = w_ff_gate.astype(BF16), w_ff_up.astype(BF16), w_ff_down.astype(BF16)
    w_router_hi, w_router_lo = _split_bf16(jnp.pad(w_router, ((0, 0), (0, 0), (0, 128 - N_EXPERTS))))

    vec = lambda a: a.reshape(depth, 1, -1)
    g_mix_pre, g_mix_post, g_ffn_pre, g_ffn_post = map(vec, (g_mix_pre, g_mix_post, g_ffn_pre, g_ffn_post))
    conv_db, conv_ln_g, conv_ln_b = map(vec, (conv_db, conv_ln_g, conv_ln_b))

    s = jnp.zeros((MOD_ROWS, d), F32).at[:n_batch].set(_silu(c)).at[n_batch].set(_silu(c_ctx))
    mod = _mod_table(s, w_mod, b_mod)
    slabs = _attn_bias_slabs(rpb)

    xs = jnp.concatenate([x.reshape(n_lat, d), ctx.reshape(n_batch * ctx_len, d)], axis=0)
    attn_w = N_HEADS * HEAD_DIM
    conv_w = conv_dw.shape[-1]
    u_col0 = 3 * attn_w
    ga_col0 = u_col0 + 2 * conv_w
    gc_col0 = ga_col0 + d

    for layer in range(depth):
        last = layer == depth - 1
        m_rows = n_lat if last else m_all
        j = layer // 2

        proj = _in_proj(xs, g_mix_pre, mod, w_in_b, layer, **geo)
        o_attn = _attention(proj, slabs, layer, ctx_len=ctx_len, with_ctx_queries=not last, **geo)
        h_conv = _conv_module(proj, conv_dw, conv_db, conv_ln_g, conv_ln_b, layer, ctx_len=ctx_len,
                              a_col=u_col0 // conv_w, g_col=u_col0 // conv_w + 1, **geo)
        y = _merge(o_attn, h_conv, w_ao_b, w_co_b, proj, layer, ga_col0=ga_col0, gc_col0=gc_col0)
        xs = _matmul_residual(y, w_o_b, xs, g_mix_post, mod, layer, gate_col=2, m_rows=m_rows,
                              w_index=layer, **geo)

        if layer % 2 == 0:
            act = _ffn_in(xs, g_ffn_pre, mod, w_fg_b, w_fu_b, layer, j, m_rows=m_rows, **geo)
            xs = _matmul_residual(act, w_fd_b, xs, g_ffn_post, mod, layer, gate_col=5, m_rows=m_rows,
                                  w_index=j, tk=2816, **geo)
        else:
            h, top = _router(xs, g_ffn_pre, mod, w_router_hi, w_router_lo, layer, j, m_rows=m_rows, **geo)
            pos, row_token, row_w, tile_expert, tile_rows = _route_tables(top, MOE_TM)
            x_sorted = jnp.take(h, row_token, axis=0, mode="clip")
            act = _moe_up(tile_expert, tile_rows, x_sorted, w_exp_gate, w_exp_up, j)
            y_sorted = _moe_down(tile_expert, tile_rows, act, w_exp_down, row_w[:, None], j)
            y0 = jnp.take(y_sorted, pos[0], axis=0, mode="clip")
            y1 = jnp.take(y_sorted, pos[1], axis=0, mode="clip")
            xs = _norm_residual(y0, y1, xs, g_ffn_post, mod, layer, gate_col=5, m_rows=m_rows,
                                in_place=not last, **geo)

    return xs[:n_lat].reshape(n_batch, seq, d)
```

```python
import functools

import numpy as np
import jax
import jax.numpy as jnp
from jax import lax
from jax.experimental import pallas as pl
from jax.experimental.pallas import tpu as pltpu

F32 = jnp.float32
BF16 = jnp.bfloat16

EPS = 1e-6
GRID_W = 64
N_HEADS = 16
HEAD_DIM = 128
CONV_K = 31
WIN_H = 8
WIN_W = 16
N_MOD = 6
N_EXPERTS = 8
MOD_ROWS = 8
ROUTE_ROWS = 8

ATT_QROWS = 4
ATT_KROWS = 12
ATT_HPB = 4
CONV_T = 256
CONV_HALO = 16
MOE_TM = 1024

VMEM_LIMIT = 56 * 1024 * 1024


def _cparams(sem):
    return pltpu.CompilerParams(dimension_semantics=sem, vmem_limit_bytes=VMEM_LIMIT)


def _silu(x):
    return x * jax.nn.sigmoid(x)


def _rms(x, g):
    return x * lax.rsqrt(jnp.mean(x * x, axis=-1, keepdims=True) + EPS) * g


def _mod_row(mod_ref, grp, col, d):
    return mod_ref[pl.ds(grp, 1), col * d:(col + 1) * d]


def _row_chunks(n_rows, rows, body):
    def step(r, carry):
        body(pl.multiple_of(r * rows, rows))
        return carry
    lax.fori_loop(0, n_rows // rows, step, 0)


def _prenorm_chunk(x, g, shift, scale1):
    return _rms(x, g) * scale1 + shift


def _prenorm_into(h_ref, x_ref, g_ref, mod_ref, grp, shift_col, scale_col, rows=128):
    d = x_ref.shape[-1]
    shift = _mod_row(mod_ref, grp, shift_col, d)
    scale1 = 1.0 + _mod_row(mod_ref, grp, scale_col, d)
    g = g_ref[...]

    def body(r0):
        x = x_ref[pl.ds(r0, rows), :]
        h_ref[pl.ds(r0, rows), :] = _prenorm_chunk(x, g, shift, scale1).astype(h_ref.dtype)

    _row_chunks(x_ref.shape[0], rows, body)


def _mod_kernel(s_ref, w_ref, b_ref, o_ref):
    o_ref[...] = jnp.dot(s_ref[...], w_ref[...], preferred_element_type=F32,
                         precision=lax.Precision.HIGHEST) + b_ref[...]


def _mod_table(s, w_mod, b_mod):
    depth, d, n = w_mod.shape
    tn = 1024
    return pl.pallas_call(
        _mod_kernel,
        out_shape=jax.ShapeDtypeStruct((depth, MOD_ROWS, n), F32),
        grid=(depth, n // tn),
        in_specs=[
            pl.BlockSpec((MOD_ROWS, d), lambda l, j: (0, 0)),
            pl.BlockSpec((None, d, tn), lambda l, j: (l, 0, j)),
            pl.BlockSpec((None, 1, tn), lambda l, j: (l, 0, j)),
        ],
        out_specs=pl.BlockSpec((None, MOD_ROWS, tn), lambda l, j: (l, 0, j)),
        compiler_params=_cparams(("parallel", "parallel")),
        name="mod_table",
    )(s, w_mod, b_mod.reshape(depth, 1, n))


def _in_proj_kernel(x_ref, g_ref, mod_ref, w_ref, o_ref, h_ref, *, tiles_per_batch, n_batch):
    i = pl.program_id(0)

    @pl.when(pl.program_id(1) == 0)
    def _():
        grp = jnp.minimum(i // tiles_per_batch, n_batch)
        _prenorm_into(h_ref, x_ref, g_ref, mod_ref, grp, 0, 1)

    o_ref[...] = jnp.dot(h_ref[...], w_ref[...], preferred_element_type=F32).astype(o_ref.dtype)


def _in_proj(xs, g, mod, w, layer, *, w_index, n_batch, seq, tm=1024, tn=2048):
    m, d = xs.shape
    n = w.shape[-1]
    kern = functools.partial(_in_proj_kernel, tiles_per_batch=seq // tm, n_batch=n_batch)
    return pl.pallas_call(
        kern,
        out_shape=jax.ShapeDtypeStruct((m, n), BF16),
        grid=(m // tm, n // tn),
        in_specs=[
            pl.BlockSpec((tm, d), lambda i, j: (i, 0)),
            pl.BlockSpec((None, 1, d), lambda i, j: (layer, 0, 0)),
            pl.BlockSpec((None, MOD_ROWS, N_MOD * d), lambda i, j: (layer, 0, 0)),
            pl.BlockSpec((None, d, tn), lambda i, j: (w_index, 0, j)),
        ],
        out_specs=pl.BlockSpec((tm, tn), lambda i, j: (i, j)),
        scratch_shapes=[pltpu.VMEM((tm, d), BF16)],
        compiler_params=_cparams(("parallel", "arbitrary")),
        name="in_proj",
    )(xs, g, mod, w)


def _attn_tables():
    rows = GRID_W
    qi = np.arange(ATT_QROWS)[:, None]
    kr = np.arange(ATT_KROWS)[None, :]
    drs, row_oks = [], []
    for i0, start in ((0, 0), (2 * ATT_QROWS, ATT_QROWS), (rows - ATT_QROWS, rows - ATT_KROWS)):
        i = i0 + qi
        r = start + kr
        r0 = np.clip(i - WIN_H // 2, 0, rows - WIN_H)
        row_oks.append((r >= r0) & (r < r0 + WIN_H))
        drs.append(np.clip(r - i + (WIN_H - 1), 0, 2 * WIN_H - 2))
    qj = np.arange(GRID_W)[:, None]
    kc = np.arange(GRID_W)[None, :]
    ws = np.clip(qj - WIN_W // 2, 0, GRID_W - WIN_W)
    col_ok = (kc >= ws) & (kc < ws + WIN_W)
    dc = np.clip(kc - qj, -(WIN_W - 1), WIN_W - 1) + (WIN_W - 1)
    return np.stack(drs).astype(np.int32), np.stack(row_oks), dc.astype(np.int32), col_ok


def _attn_pair_plan():
    dr, row_ok, _, _ = _attn_tables()
    entries, plan = [], []
    for pat in range(dr.shape[0]):
        per_q = []
        for qi in range(ATT_QROWS):
            per_p = []
            for p in range(ATT_KROWS // 2):
                halves = tuple(int(dr[pat, qi, kr]) if row_ok[pat, qi, kr] else None for kr in (2 * p, 2 * p + 1))
                if halves == (None, None):
                    per_p.append(None)
                    continue
                if halves not in entries:
                    entries.append(halves)
                per_p.append(entries.index(halves))
            per_q.append(per_p)
        plan.append(per_q)
    return entries, plan


_LOG2E = float(np.log2(np.e))


def _attn_bias_slabs(rpb):
    _, _, dc, col_ok = _attn_tables()
    entries, _ = _attn_pair_plan()
    onehot = (dc[:, :, None] == np.arange(2 * WIN_W - 1)[None, None, :]).astype(np.float32)
    t = jnp.einsum("lhdk,jck->lhdjc", rpb.astype(F32), onehot, precision=lax.Precision.HIGHEST)
    t = jnp.where(col_ok, t * _LOG2E, -jnp.inf)
    masked = jnp.full(t.shape[:2] + t.shape[3:], -jnp.inf, F32)
    half = lambda d: masked if d is None else t[:, :, d]
    return jnp.stack([jnp.concatenate([half(a), half(b)], axis=-1) for a, b in entries], axis=2)


def _attn_kernel(q_ref, k_ref, v_ref, kc_ref, vc_ref, slab_ref, o_ref, bias_ref, *,
                 n_qblocks, max_start, with_ctx_queries):
    rb = pl.program_id(2)
    nk = ATT_KROWS * GRID_W
    scale2 = HEAD_DIM ** -0.5 * _LOG2E
    dn = (((1,), (1,)), ((), ()))
    _, plan = _attn_pair_plan()

    def scores(q, k):
        return lax.dot_general(q, k, dn, preferred_element_type=F32) * scale2

    def assemble(pattern):
        masked = jnp.full((GRID_W, 2 * GRID_W), -jnp.inf, F32)
        for hh in range(ATT_HPB):
            for qi in range(ATT_QROWS):
                for p, e in enumerate(plan[pattern][qi]):
                    blk = masked if e is None else slab_ref[hh, e]
                    bias_ref[hh, qi * GRID_W:(qi + 1) * GRID_W, 2 * p * GRID_W:2 * (p + 1) * GRID_W] = blk

    for pattern, first_rb in ((0, 0), (1, 1), (2, n_qblocks - 1)):
        pl.when(rb == first_rb)(functools.partial(assemble, pattern))

    @pl.when(rb < n_qblocks)
    def _():
        start = jnp.clip(rb * ATT_QROWS - WIN_H // 2, 0, max_start)
        tok0 = pl.multiple_of(start * GRID_W, GRID_W)
        heads = [slice(hh * HEAD_DIM, (hh + 1) * HEAD_DIM) for hh in range(ATT_HPB)]

        s_all = [(scores(q_ref[:, hs], k_ref[pl.ds(tok0, nk), hs]) + bias_ref[hh], scores(q_ref[:, hs], kc_ref[:, hs]))
                 for hh, hs in enumerate(heads)]
        for hs, (s_loc, s_ctx) in zip(heads, s_all):
            m = jnp.maximum(jnp.max(s_loc, axis=-1, keepdims=True), jnp.max(s_ctx, axis=-1, keepdims=True))
            p_loc = jnp.exp2(s_loc - m)
            p_ctx = jnp.exp2(s_ctx - m)
            l = jnp.sum(p_loc, axis=-1, keepdims=True) + jnp.sum(p_ctx, axis=-1, keepdims=True)
            o = (jnp.dot(p_loc.astype(BF16), v_ref[pl.ds(tok0, nk), hs], preferred_element_type=F32)
                 + jnp.dot(p_ctx.astype(BF16), vc_ref[:, hs], preferred_element_type=F32))
            o_ref[:, hs] = (o / l).astype(o_ref.dtype)

    if with_ctx_queries:
        @pl.when(rb == n_qblocks)
        def _():
            for hh in range(ATT_HPB):
                hs = slice(hh * HEAD_DIM, (hh + 1) * HEAD_DIM)
                s_ctx = scores(q_ref[:, hs], kc_ref[:, hs])
                p_ctx = jnp.exp2(s_ctx - jnp.max(s_ctx, axis=-1, keepdims=True))
                l = jnp.sum(p_ctx, axis=-1, keepdims=True)
                o = jnp.dot(p_ctx.astype(BF16), vc_ref[:, hs], preferred_element_type=F32)
                o_ref[:, hs] = (o / l).astype(o_ref.dtype)


def _attention(proj, slabs, layer, *, n_batch, seq, ctx_len, with_ctx_queries):
    m = proj.shape[0]
    nq = ATT_QROWS * GRID_W
    assert nq == ctx_len, "context queries reuse the latent query-block shape"
    rows = seq // GRID_W
    n_qblocks = rows // ATT_QROWS
    ctx_blk0 = (n_batch * seq) // ctx_len
    n_hg = N_HEADS // ATT_HPB
    wblk = ATT_HPB * HEAD_DIM

    def qrow(b, rb):
        return jnp.where(rb == n_qblocks, ctx_blk0 + b, b * n_qblocks + rb)

    assert n_qblocks >= 3, "first / interior / last bias patterns need distinct query blocks"
    n_entries = slabs.shape[2]
    kern = functools.partial(_attn_kernel, n_qblocks=n_qblocks, max_start=rows - ATT_KROWS,
                             with_ctx_queries=with_ctx_queries)
    return pl.pallas_call(
        kern,
        out_shape=jax.ShapeDtypeStruct((m if with_ctx_queries else n_batch * seq, N_HEADS * HEAD_DIM), BF16),
        grid=(n_batch, n_hg, n_qblocks + (1 if with_ctx_queries else 0)),
        in_specs=[
            pl.BlockSpec((nq, wblk), lambda b, h, rb: (qrow(b, rb), h)),
            pl.BlockSpec((seq, wblk), lambda b, h, rb: (b, n_hg + h)),
            pl.BlockSpec((seq, wblk), lambda b, h, rb: (b, 2 * n_hg + h)),
            pl.BlockSpec((ctx_len, wblk), lambda b, h, rb: (ctx_blk0 + b, n_hg + h)),
            pl.BlockSpec((ctx_len, wblk), lambda b, h, rb: (ctx_blk0 + b, 2 * n_hg + h)),
            pl.BlockSpec((None, ATT_HPB, n_entries, GRID_W, 2 * GRID_W), lambda b, h, rb: (layer, h, 0, 0, 0)),
        ],
        out_specs=pl.BlockSpec((nq, wblk), lambda b, h, rb: (qrow(b, rb), h)),
        scratch_shapes=[pltpu.VMEM((ATT_HPB, nq, ATT_KROWS * GRID_W), F32)],
        compiler_params=_cparams(("parallel", "parallel", "arbitrary")),
        name="attention",
    )(proj, proj, proj, proj, proj, slabs)


def _conv_kernel(a_ref, g_ref, ap_ref, gp_ref, an_ref, gn_ref, w_ref, b_ref, lg_ref, lb_ref,
                 o_ref, hext_ref, y_ref, *, tiles_per_seq, n_latent_tiles):
    i = pl.program_id(0)
    t = CONV_T
    halo = CONV_HALO
    c = a_ref.shape[-1]
    is_ctx = i >= n_latent_tiles
    pos = i % tiles_per_seq
    at_start = jnp.logical_or(is_ctx, pos == 0)
    at_end = jnp.logical_or(is_ctx, pos == tiles_per_seq - 1)

    def glu(a, g):
        return a.astype(F32) * jax.nn.sigmoid(g.astype(F32))

    lanes = 128
    rchunk = 64
    off = halo - CONV_K // 2

    hext_ref[0:halo, :] = jnp.where(at_start, 0.0, glu(ap_ref[...], gp_ref[...]))
    hext_ref[halo + t:halo + t + halo, :] = jnp.where(at_end, 0.0, glu(an_ref[...], gn_ref[...]))

    def glu_body(r0):
        dst = pl.multiple_of(r0 + halo, halo)
        hext_ref[pl.ds(dst, rchunk), :] = glu(a_ref[pl.ds(r0, rchunk), :], g_ref[pl.ds(r0, rchunk), :])

    _row_chunks(t, rchunk, glu_body)

    def chunk_body(ci, carry):
        c0 = pl.multiple_of(ci * lanes, lanes)
        wts = w_ref[:, pl.ds(c0, lanes)]
        bias = b_ref[:, pl.ds(c0, lanes)]
        for r0 in range(0, t, rchunk):
            acc = bias
            for res in range(8):
                n_rows = rchunk + (8 if res else 0)
                part = None
                for base in range(0, CONV_K + off, 8):
                    k = base + res - off
                    if 0 <= k < CONV_K:
                        term = wts[k:k + 1, :] * hext_ref[r0 + base:r0 + base + n_rows, pl.ds(c0, lanes)]
                        part = term if part is None else part + term
                if part is not None:
                    acc = acc + part[res:res + rchunk, :]
            y_ref[r0:r0 + rchunk, pl.ds(c0, lanes)] = acc
        return carry

    lax.fori_loop(0, c // lanes, chunk_body, 0)

    ln_g = lg_ref[...]
    ln_b = lb_ref[...]

    def ln_body(r0):
        y = y_ref[pl.ds(r0, rchunk), :]
        mu = jnp.mean(y, axis=-1, keepdims=True)
        yc = y - mu
        var = jnp.mean(yc * yc, axis=-1, keepdims=True)
        z = yc * lax.rsqrt(var + EPS) * ln_g + ln_b
        o_ref[pl.ds(r0, rchunk), :] = _silu(z).astype(o_ref.dtype)

    _row_chunks(t, rchunk, ln_body)


def _conv_module(proj, w_dw, b_dw, ln_g, ln_b, layer, *, n_batch, seq, ctx_len, a_col, g_col, m_rows):
    m = m_rows
    c = w_dw.shape[-1]
    t = CONV_T
    assert ctx_len == t and seq % t == 0
    hb = t // CONV_HALO
    last_hblk = proj.shape[0] // CONV_HALO - 1
    n_tiles = m // t
    kern = functools.partial(_conv_kernel, tiles_per_seq=seq // t, n_latent_tiles=(n_batch * seq) // t)

    def prev(i):
        return jnp.maximum(i * hb - 1, 0)

    def nxt(i):
        return jnp.minimum(i * hb + hb, last_hblk)

    vec = lambda: pl.BlockSpec((None, 1, c), lambda i: (layer, 0, 0))
    return pl.pallas_call(
        kern,
        out_shape=jax.ShapeDtypeStruct((m, c), BF16),
        grid=(n_tiles,),
        in_specs=[
            pl.BlockSpec((t, c), lambda i: (i, a_col)),
            pl.BlockSpec((t, c), lambda i: (i, g_col)),
            pl.BlockSpec((CONV_HALO, c), lambda i: (prev(i), a_col)),
            pl.BlockSpec((CONV_HALO, c), lambda i: (prev(i), g_col)),
            pl.BlockSpec((CONV_HALO, c), lambda i: (nxt(i), a_col)),
            pl.BlockSpec((CONV_HALO, c), lambda i: (nxt(i), g_col)),
            pl.BlockSpec((None, CONV_K, c), lambda i: (layer, 0, 0)),
            vec(), vec(), vec(),
        ],
        out_specs=pl.BlockSpec((t, c), lambda i: (i, 0)),
        scratch_shapes=[pltpu.VMEM((t + 2 * CONV_HALO, c), F32), pltpu.VMEM((t, c), F32)],
        compiler_params=_cparams(("parallel",)),
        name="conv_module",
    )(proj, proj, proj, proj, proj, proj, w_dw, b_dw, ln_g, ln_b)


def _merge_kernel(oa_ref, hc_ref, wa_ref, wc_ref, ga_ref, gc_ref, o_ref):
    ya = jnp.dot(oa_ref[...], wa_ref[...], preferred_element_type=F32)
    yc = jnp.dot(hc_ref[...], wc_ref[...], preferred_element_type=F32)
    y = jax.nn.sigmoid(ga_ref[...].astype(F32)) * ya + jax.nn.sigmoid(gc_ref[...].astype(F32)) * yc
    o_ref[...] = y.astype(o_ref.dtype)


def _merge(o_attn, h_conv, wa, wc, proj, layer, *, ga_col0, gc_col0, tm=512, tn=1024):
    m, k = o_attn.shape
    n = wa.shape[-1]
    return pl.pallas_call(
        _merge_kernel,
        out_shape=jax.ShapeDtypeStruct((m, n), BF16),
        grid=(n // tn, m // tm),
        in_specs=[
            pl.BlockSpec((tm, k), lambda j, i: (i, 0)),
            pl.BlockSpec((tm, k), lambda j, i: (i, 0)),
            pl.BlockSpec((None, k, tn), lambda j, i: (layer, 0, j)),
            pl.BlockSpec((None, k, tn), lambda j, i: (layer, 0, j)),
            pl.BlockSpec((tm, tn), lambda j, i: (i, ga_col0 // tn + j)),
            pl.BlockSpec((tm, tn), lambda j, i: (i, gc_col0 // tn + j)),
        ],
        out_specs=pl.BlockSpec((tm, tn), lambda j, i: (i, j)),
        compiler_params=_cparams(("parallel", "parallel")),
        name="merge",
    )(o_attn, h_conv, wa, wc, proj, proj)


def _post_residual(z_chunk, x_ref, g_ref, mod_ref, grp, gate_col, o_ref, rows=128):
    d = x_ref.shape[-1]
    gate = _mod_row(mod_ref, grp, gate_col, d)
    g = g_ref[...]

    def body(r0):
        o_ref[pl.ds(r0, rows), :] = x_ref[pl.ds(r0, rows), :] + gate * _rms(z_chunk(r0), g)

    _row_chunks(x_ref.shape[0], rows, body)


def _matmul_res_kernel(a_ref, w_ref, x_ref, g_ref, mod_ref, o_ref, acc0_ref, acc1_ref, *,
                       tiles_per_batch, n_batch, gate_col, nk, n_tiles):
    i = pl.program_id(0)
    kk = pl.program_id(1)
    tm, d = x_ref.shape
    rows = tm // nk
    gate = _mod_row(mod_ref, jnp.minimum(jnp.maximum(i - 1, 0) // tiles_per_batch, n_batch), gate_col, d)

    @pl.when(jnp.logical_and(i == 0, kk == 0))
    def _():
        acc1_ref[...] = jnp.zeros_like(acc1_ref)

    for parity, (acc_w, acc_r) in enumerate(((acc0_ref, acc1_ref), (acc1_ref, acc0_ref))):
        for k_step in range(nk):
            @pl.when(jnp.logical_and(i % 2 == parity, kk == k_step))
            def _(acc_w=acc_w, acc_r=acc_r, k_step=k_step):
                part = jnp.dot(a_ref[...], w_ref[...], preferred_element_type=F32)
                if k_step == 0:
                    acc_w[...] = part
                else:
                    acc_w[...] += part
                sub = min(rows, 128)
                for r0 in range(k_step * rows, (k_step + 1) * rows, sub):
                    z = acc_r[r0:r0 + sub, :]
                    o_ref[r0:r0 + sub, :] = x_ref[r0:r0 + sub, :] + gate * _rms(z, g_ref[...])


def _matmul_residual(a, w, xs, g, mod, layer, *, gate_col, n_batch, seq, m_rows, w_index, tm=512, tk=None):
    k = a.shape[1]
    d = xs.shape[1]
    tk = k if tk is None else tk
    nk = k // tk
    n_tiles = m_rows // tm
    kern = functools.partial(_matmul_res_kernel, tiles_per_batch=seq // tm, n_batch=n_batch,
                             gate_col=gate_col, nk=nk, n_tiles=n_tiles)
    cur = lambda i: jnp.minimum(i, n_tiles - 1)
    prev = lambda i: jnp.maximum(i - 1, 0)
    return pl.pallas_call(
        kern,
        out_shape=jax.ShapeDtypeStruct(xs.shape, F32),
        grid=(n_tiles + 1, nk),
        in_specs=[
            pl.BlockSpec((tm, tk), lambda i, kk: (cur(i), kk)),
            pl.BlockSpec((None, tk, d), lambda i, kk: (w_index, kk, 0)),
            pl.BlockSpec((tm, d), lambda i, kk: (prev(i), 0)),
            pl.BlockSpec((None, 1, d), lambda i, kk: (layer, 0, 0)),
            pl.BlockSpec((None, MOD_ROWS, N_MOD * d), lambda i, kk: (layer, 0, 0)),
        ],
        out_specs=pl.BlockSpec((tm, d), lambda i, kk: (prev(i), 0)),
        scratch_shapes=[pltpu.VMEM((tm, d), F32), pltpu.VMEM((tm, d), F32)],
        input_output_aliases={2: 0},
        compiler_params=_cparams(("arbitrary", "arbitrary")),
        name="matmul_residual",
    )(a, w, xs, g, mod)


def _ffn_in_kernel(x_ref, g_ref, mod_ref, wg_ref, wu_ref, o_ref, h_ref, *, tiles_per_batch, n_batch):
    i = pl.program_id(0)

    @pl.when(pl.program_id(1) == 0)
    def _():
        grp = jnp.minimum(i // tiles_per_batch, n_batch)
        _prenorm_into(h_ref, x_ref, g_ref, mod_ref, grp, 3, 4)

    h = h_ref[...]
    a = jnp.dot(h, wg_ref[...], preferred_element_type=F32)
    u = jnp.dot(h, wu_ref[...], preferred_element_type=F32)
    o_ref[...] = (_silu(a) * u).astype(o_ref.dtype)


def _ffn_in(xs, g, mod, wg, wu, layer, j_dense, *, n_batch, seq, m_rows, tm=1024, tn=512):
    d = xs.shape[1]
    f = wg.shape[-1]
    kern = functools.partial(_ffn_in_kernel, tiles_per_batch=seq // tm, n_batch=n_batch)
    return pl.pallas_call(
        kern,
        out_shape=jax.ShapeDtypeStruct((m_rows, f), BF16),
        grid=(m_rows // tm, f // tn),
        in_specs=[
            pl.BlockSpec((tm, d), lambda i, j: (i, 0)),
            pl.BlockSpec((None, 1, d), lambda i, j: (layer, 0, 0)),
            pl.BlockSpec((None, MOD_ROWS, N_MOD * d), lambda i, j: (layer, 0, 0)),
            pl.BlockSpec((None, d, tn), lambda i, j: (j_dense, 0, j)),
            pl.BlockSpec((None, d, tn), lambda i, j: (j_dense, 0, j)),
        ],
        out_specs=pl.BlockSpec((tm, tn), lambda i, j: (i, j)),
        scratch_shapes=[pltpu.VMEM((tm, d), BF16)],
        compiler_params=_cparams(("parallel", "arbitrary")),
        name="ffn_in",
    )(xs, g, mod, wg, wu)


def _split_bf16(a):
    hi = a.astype(BF16)
    return hi, (a - hi.astype(F32)).astype(BF16)


def _router_kernel(x_ref, g_ref, mod_ref, wh_ref, wl_ref, h_ref, top_ref, *, tiles_per_batch, n_batch):
    i = pl.program_id(0)
    grp = jnp.minimum(i // tiles_per_batch, n_batch)
    d = x_ref.shape[-1]
    shift = _mod_row(mod_ref, grp, 3, d)
    scale1 = 1.0 + _mod_row(mod_ref, grp, 4, d)
    g = g_ref[...]
    rows = 256

    def body(r0):
        h = _prenorm_chunk(x_ref[pl.ds(r0, rows), :], g, shift, scale1)
        h_ref[pl.ds(r0, rows), :] = h.astype(h_ref.dtype)
        h_hi, h_lo = _split_bf16(h)
        dot = functools.partial(jnp.dot, preferred_element_type=F32)
        logits = dot(h_hi, wh_ref[...]) + (dot(h_hi, wl_ref[...]) + dot(h_lo, wh_ref[...]))
        lane =lax.broadcasted_iota(jnp.int32, logits.shape, 1)
        big = jnp.int32(logits.shape[-1])
        logits = jnp.where(lane < N_EXPERTS, logits, -jnp.inf)
        m1 = jnp.max(logits, axis=-1, keepdims=True)
        i1 = jnp.min(jnp.where(logits == m1, lane, big), axis=-1, keepdims=True)
        rest = jnp.where(lane == i1, -jnp.inf, logits)
        m2 = jnp.max(rest, axis=-1, keepdims=True)
        i2 = jnp.min(jnp.where(rest == m2, lane, big), axis=-1, keepdims=True)
        e2 = jnp.exp(m2 - m1)
        w1 = 1.0 / (1.0 + e2)
        w2 = e2 / (1.0 + e2)
        packed = jnp.where(lane == 0, i1.astype(F32), jnp.where(lane == 1, i2.astype(F32),
                           jnp.where(lane == 2, w1, jnp.where(lane == 3, w2, 0.0))))
        top_ref[:, pl.ds(r0, rows)] = jnp.transpose(packed)[:ROUTE_ROWS, :]

    _row_chunks(x_ref.shape[0], rows, body)


def _router(xs, g, mod, w_hi, w_lo, layer, j_moe, *, n_batch, seq, m_rows, tm=512):
    d = xs.shape[1]
    lanes = w_hi.shape[-1]
    kern = functools.partial(_router_kernel, tiles_per_batch=seq // tm, n_batch=n_batch)
    return pl.pallas_call(
        kern,
        out_shape=(jax.ShapeDtypeStruct((m_rows, d), BF16),
                   jax.ShapeDtypeStruct((ROUTE_ROWS, m_rows), F32)),
        grid=(m_rows // tm,),
        in_specs=[
            pl.BlockSpec((tm, d), lambda i: (i, 0)),
            pl.BlockSpec((None, 1, d), lambda i: (layer, 0, 0)),
            pl.BlockSpec((None, MOD_ROWS, N_MOD * d), lambda i: (layer, 0, 0)),
            pl.BlockSpec((None, d, lanes), lambda i: (j_moe, 0, 0)),
            pl.BlockSpec((None, d, lanes), lambda i: (j_moe, 0, 0)),
        ],
        out_specs=(pl.BlockSpec((tm, d), lambda i: (i, 0)),
                   pl.BlockSpec((ROUTE_ROWS, tm), lambda i: (0, i))),
        compiler_params=_cparams(("parallel",)),
        name="router",
    )(xs, g, mod, w_hi, w_lo)


def _expert_block_is_new(te_ref, t):
    return jnp.logical_or(t == 0, te_ref[t] != te_ref[jnp.maximum(t - 1, 0)])


def _tile_cases(rows_ref, t, tm, full, half, empty):
    rows = rows_ref[t]
    pl.when(rows > tm // 2)(full)
    pl.when(jnp.logical_and(rows > 0, rows <= tm // 2))(half)
    pl.when(rows == 0)(empty)


def _moe_up_kernel(te_ref, rows_ref, x_ref, wg_ref, wu_ref, o_ref, wgb_ref, wub_ref):
    t = pl.program_id(1)
    tm = x_ref.shape[0]

    @pl.when(_expert_block_is_new(te_ref, t))
    def _():
        wgb_ref[...] = wg_ref[...].astype(BF16)
        wub_ref[...] = wu_ref[...].astype(BF16)

    def compute(n_rows):
        x = x_ref[:n_rows, :]
        a = jnp.dot(x, wgb_ref[...], preferred_element_type=F32)
        u = jnp.dot(x, wub_ref[...], preferred_element_type=F32)
        o_ref[:n_rows, :] = (_silu(a) * u).astype(o_ref.dtype)

    def half():
        compute(tm // 2)
        o_ref[tm // 2:, :] = jnp.zeros((tm - tm // 2, o_ref.shape[1]), o_ref.dtype)

    def empty():
        o_ref[...] = jnp.zeros_like(o_ref)

    _tile_cases(rows_ref, t, tm, functools.partial(compute, tm), half, empty)


def _moe_up(tile_expert, tile_rows, x_sorted, wg, wu, j_moe, *, tn=512):
    r, d = x_sorted.shape
    f = wg.shape[-1]
    tm = MOE_TM
    grid_spec = pltpu.PrefetchScalarGridSpec(
        num_scalar_prefetch=2,
        grid=(f // tn, r // tm),
        in_specs=[
            pl.BlockSpec((tm, d), lambda j, t, te, nu: (t, 0)),
            pl.BlockSpec((None, None, d, tn), lambda j, t, te, nu: (j_moe, te[t], 0, j)),
            pl.BlockSpec((None, None, d, tn), lambda j, t, te, nu: (j_moe, te[t], 0, j)),
        ],
        out_specs=pl.BlockSpec((None, tm, tn), lambda j, t, te, nu: (j, t, 0)),
        scratch_shapes=[pltpu.VMEM((d, tn), BF16), pltpu.VMEM((d, tn), BF16)],
    )
    return pl.pallas_call(
        _moe_up_kernel,
        out_shape=jax.ShapeDtypeStruct((f // tn, r, tn), BF16),
        grid_spec=grid_spec,
        compiler_params=_cparams(("arbitrary", "arbitrary")),
        name="moe_up",
    )(tile_expert, tile_rows, x_sorted, wg, wu)


def _moe_down_kernel(te_ref, rows_ref, a_ref, wd_ref, rw_ref, o_ref, wdb_ref, *, tiles_per_up_tile):
    t = pl.program_id(1)
    n_chunks, tm, tk = a_ref.shape
    up_tile = t // tiles_per_up_tile
    rows = jnp.clip(rows_ref[up_tile] - (t % tiles_per_up_tile) * tm, 0, tm)

    @pl.when(_expert_block_is_new(te_ref, up_tile) & (t % tiles_per_up_tile == 0))
    def _():
        wdb_ref[...] = wd_ref[...].astype(BF16)

    @pl.when(rows > 0)
    def _():
        y = jnp.dot(a_ref[0], wdb_ref[0:tk, :], preferred_element_type=F32)
        for c in range(1, n_chunks):
            y = y + jnp.dot(a_ref[c], wdb_ref[c * tk:(c + 1) * tk, :], preferred_element_type=F32)
        o_ref[...] = (y * rw_ref[...]).astype(o_ref.dtype)

    @pl.when(rows == 0)
    def _():
        o_ref[...] = jnp.zeros_like(o_ref)


def _moe_down(tile_expert, tile_rows, act, wd, row_w, j_moe, *, tm=512, tn=1024):
    n_chunks, r, tk = act.shape
    f = n_chunks * tk
    d = wd.shape[-1]
    per_up = MOE_TM // tm
    kern = functools.partial(_moe_down_kernel, tiles_per_up_tile=per_up)
    grid_spec = pltpu.PrefetchScalarGridSpec(
        num_scalar_prefetch=2,
        grid=(d // tn, r // tm),
        in_specs=[
            pl.BlockSpec((n_chunks, tm, tk), lambda j, t, te, nu: (0, t, 0)),
            pl.BlockSpec((None, None, f, tn), lambda j, t, te, nu: (j_moe, te[t // per_up], 0, j)),
            pl.BlockSpec((tm, 1), lambda j, t, te, nu: (t, 0)),
        ],
        out_specs=pl.BlockSpec((tm, tn), lambda j, t, te, nu: (t, j)),
        scratch_shapes=[pltpu.VMEM((f, tn), BF16)],
    )
    return pl.pallas_call(
        kern,
        out_shape=jax.ShapeDtypeStruct((r, d), BF16),
        grid_spec=grid_spec,
        compiler_params=_cparams(("arbitrary", "arbitrary")),
        name="moe_down",
    )(tile_expert, tile_rows, act, wd, row_w)


def _norm_res_kernel(y0_ref, y1_ref, x_ref, g_ref, mod_ref, o_ref, *, tiles_per_batch, n_batch, gate_col):
    i = pl.program_id(0)
    grp = jnp.minimum(i // tiles_per_batch, n_batch)
    rows = 128
    _post_residual(lambda r0: y0_ref[pl.ds(r0, rows), :].astype(F32) + y1_ref[pl.ds(r0, rows), :].astype(F32),
                   x_ref, g_ref, mod_ref, grp, gate_col, o_ref, rows)


def _norm_residual(y0, y1, xs, g, mod, layer, *, gate_col, n_batch, seq, m_rows, in_place, tm=512):
    d = xs.shape[1]
    kern = functools.partial(_norm_res_kernel, tiles_per_batch=seq // tm, n_batch=n_batch, gate_col=gate_col)
    row = lambda: pl.BlockSpec((tm, d), lambda i: (i, 0))
    return pl.pallas_call(
        kern,
        out_shape=jax.ShapeDtypeStruct(xs.shape if in_place else (m_rows, d), F32),
        grid=(m_rows // tm,),
        in_specs=[row(), row(), row(),
                  pl.BlockSpec((None, 1, d), lambda i: (layer, 0, 0)),
                  pl.BlockSpec((None, MOD_ROWS, N_MOD * d), lambda i: (layer, 0, 0))],
        out_specs=row(),
        input_output_aliases={2: 0} if in_place else {},
        compiler_params=_cparams(("parallel",)),
        name="norm_residual",
    )(y0, y1, xs, g, mod)


RANK_CHUNK = 512


def _rank_kernel(e_ref, tri_ref, rank_ref, cnt_ref, carry_ref):
    @pl.when(pl.program_id(0) == 0)
    def _():
        carry_ref[...] = jnp.zeros_like(carry_ref)

    e = e_ref[...]
    expert = lax.broadcasted_iota(jnp.int32, (N_EXPERTS, e.shape[1]), 0)
    onehot = (expert == e).astype(F32)
    local = jnp.dot(onehot.astype(BF16), tri_ref[...], preferred_element_type=F32)
    carry = carry_ref[...]
    csum = local + carry[:, :1]
    rank_ref[...] = (jnp.sum(csum * onehot, axis=0, keepdims=True) - 1.0).astype(jnp.int32)
    carry = carry + jnp.sum(onehot, axis=1, keepdims=True)
    carry_ref[...] = carry
    cnt_ref[...] = carry.astype(jnp.int32)


def _pair_ranks(flat_e):
    n2 = flat_e.shape[0]
    c = RANK_CHUNK
    tri = jnp.asarray(np.triu(np.ones((c, c), np.float32)), BF16)
    rank, cnt = pl.pallas_call(
        _rank_kernel,
        out_shape=(jax.ShapeDtypeStruct((1, n2), jnp.int32), jax.ShapeDtypeStruct((N_EXPERTS, 128), jnp.int32)),
        grid=(n2 // c,),
        in_specs=[pl.BlockSpec((1, c), lambda s: (0, s)), pl.BlockSpec((c, c), lambda s: (0, 0))],
        out_specs=(pl.BlockSpec((1, c), lambda s: (0, s)), pl.BlockSpec((N_EXPERTS, 128), lambda s: (0, 0))),
        scratch_shapes=[pltpu.VMEM((N_EXPERTS, 128), F32)],
        compiler_params=_cparams(("arbitrary",)),
        name="pair_ranks",
    )(flat_e.reshape(1, n2), tri)
    return rank.reshape(n2), cnt[:, 0]


def _route_tables(top, tm):
    n = top.shape[1]
    flat_e = top[0:2].astype(jnp.int32).reshape(-1)
    flat_w = top[2:4].reshape(-1)
    rank, counts = _pair_ranks(flat_e)
    padded = ((counts + tm - 1) // tm) * tm
    ends = jnp.cumsum(padded)
    starts = ends - padded
    pos = starts[flat_e] + rank
    r_pad = 2 * n + N_EXPERTS * tm
    n_tiles = r_pad // tm
    row_pair = jnp.full((r_pad,), 2 * n, jnp.int32).at[pos].set(jnp.arange(2 * n, dtype=jnp.int32))
    is_real = row_pair < 2 * n
    spread = jnp.arange(r_pad, dtype=jnp.int32) % (2 * n)
    row_pair = jnp.where(is_real, row_pair, spread)
    row_token = row_pair % n
    row_w = jnp.where(is_real, flat_w[row_pair], 0.0)
    n_used = (ends[-1] // tm).astype(jnp.int32)
    tile_start = jnp.arange(n_tiles, dtype=jnp.int32) * tm
    tile_expert = jnp.sum((tile_start[:, None] >= ends[None, :]).astype(jnp.int32), axis=1)
    tile_expert = jnp.minimum(tile_expert, N_EXPERTS - 1)
    tile_rows = jnp.clip(counts[tile_expert] - (tile_start - starts[tile_expert]), 0, tm)
    tile_rows = jnp.where(jnp.arange(n_tiles) < n_used, tile_rows, 0)
    last = tile_expert[jnp.maximum(n_used - 1, 0)]
    tile_expert = jnp.where(jnp.arange(n_tiles) < n_used, tile_expert, last)
    return (pos.reshape(2, n), row_token, row_w, tile_expert.astype(jnp.int32), tile_rows.astype(jnp.int32))


def kernel(x, c, ctx, c_ctx, w_mod, b_mod, g_mix_pre, g_mix_post, g_ffn_pre, g_ffn_post, w_in, rpb,
           w_attn_out, conv_dw, conv_db, conv_ln_g, conv_ln_b, w_conv_out, w_out, w_ff_gate, w_ff_up,
           w_ff_down, w_router, w_exp_gate, w_exp_up, w_exp_down):
    n_batch, seq, d = x.shape
    ctx_len = ctx.shape[1]
    depth = w_mod.shape[0]
    n_lat = n_batch * seq
    m_all = n_lat + n_batch * ctx_len
    assert seq == GRID_W * GRID_W and d == N_HEADS * HEAD_DIM and n_batch + 1 <= MOD_ROWS
    geo = dict(n_batch=n_batch, seq=seq)

    per_layer = lambda w: [w[l:l + 1].astype(BF16) for l in range(w.shape[0])]
    w_in_b, w_ao_b, w_co_b, w_o_b = map(per_layer, (w_in, w_attn_out, w_conv_out, w_out))
    w_fg_b, w_fu_b, w_fd_b = map(per_layer, (w_ff_gate, w_ff_up, w_ff_down))
    w_router_hi, w_router_lo = _split_bf16(jnp.pad(w_router, ((0, 0), (0, 0), (0, 128 - N_EXPERTS))))

    vec = lambda a: a.reshape(depth, 1, -1)
    g_mix_pre, g_mix_post, g_ffn_pre, g_ffn_post = map(vec, (g_mix_pre, g_mix_post, g_ffn_pre, g_ffn_post))
    conv_db, conv_ln_g, conv_ln_b = map(vec, (conv_db, conv_ln_g, conv_ln_b))

    s = jnp.zeros((MOD_ROWS, d), F32).at[:n_batch].set(_silu(c)).at[n_batch].set(_silu(c_ctx))
    mod = _mod_table(s, w_mod, b_mod)
    slabs = _attn_bias_slabs(rpb)

    xs = jnp.concatenate([x.reshape(n_lat, d), ctx.reshape(n_batch * ctx_len, d)], axis=0)
    attn_w = N_HEADS * HEAD_DIM
    conv_w = conv_dw.shape[-1]
    u_col0 = 3 * attn_w
    ga_col0 = u_col0 + 2 * conv_w
    gc_col0 = ga_col0 + d

    for layer in range(depth):
        last = layer == depth - 1
        m_rows = n_lat if last else m_all
        j = layer // 2

        proj = _in_proj(xs, g_mix_pre, mod, w_in_b[layer], layer, w_index=0, **geo)
        o_attn = _attention(proj, slabs, layer, ctx_len=ctx_len, with_ctx_queries=not last, **geo)
        h_conv = _conv_module(proj, conv_dw, conv_db, conv_ln_g, conv_ln_b, layer, ctx_len=ctx_len,
                              a_col=u_col0 // conv_w, g_col=u_col0 // conv_w + 1, m_rows=m_rows, **geo)
        y = _merge(o_attn, h_conv, w_ao_b[layer], w_co_b[layer], proj, 0, ga_col0=ga_col0, gc_col0=gc_col0)
        xs = _matmul_residual(y, w_o_b[layer], xs, g_mix_post, mod, layer, gate_col=2, m_rows=m_rows,
                              w_index=0, **geo)

        if layer % 2 == 0:
            act = _ffn_in(xs, g_ffn_pre, mod, w_fg_b[j], w_fu_b[j], layer, 0, m_rows=m_rows, **geo)
            xs = _matmul_residual(act, w_fd_b[j], xs, g_ffn_post, mod, layer, gate_col=5, m_rows=m_rows,
                                  w_index=0, tk=2816, **geo)
        else:
            h, top = _router(xs, g_ffn_pre, mod, w_router_hi, w_router_lo, layer, j, m_rows=m_rows, **geo)
            pos, row_token, row_w, tile_expert, tile_rows = _route_tables(top, MOE_TM)
            x_sorted = jnp.take(h, row_token, axis=0, mode="clip")
            act = _moe_up(tile_expert, tile_rows, x_sorted, w_exp_gate, w_exp_up, j)
            y_sorted = _moe_down(tile_expert, tile_rows, act, w_exp_down, row_w[:, None], j)
            y0 = jnp.take(y_sorted, pos[0], axis=0, mode="clip")
            y1 = jnp.take(y_sorted, pos[1], axis=0, mode="clip")
            xs = _norm_residual(y0, y1, xs, g_ffn_post, mod, layer, gate_col=5, m_rows=m_rows,
                                in_place=not last, **geo)

    return xs[:n_lat].reshape(n_batch, seq, d)
```

```python
import functools

import numpy as np
import jax
import jax.numpy as jnp
from jax import lax
from jax.experimental import pallas as pl
from jax.experimental.pallas import tpu as pltpu

F32 = jnp.float32
BF16 = jnp.bfloat16

EPS = 1e-6
GRID_W = 64
N_HEADS = 16
HEAD_DIM = 128
CONV_K = 31
WIN_H = 8
WIN_W = 16
N_MOD = 6
N_EXPERTS = 8
MOD_ROWS = 8
ROUTE_ROWS = 8

ATT_QROWS = 4
ATT_KROWS = 12
ATT_HPB = 4
CONV_T = 256
CONV_HALO = 16
MOE_TM = 1024

VMEM_LIMIT = 56 * 1024 * 1024


def _cparams(sem):
    return pltpu.CompilerParams(dimension_semantics=sem, vmem_limit_bytes=VMEM_LIMIT)


def _silu(x):
    return x * jax.nn.sigmoid(x)


def _rms(x, g):
    return x * lax.rsqrt(jnp.mean(x * x, axis=-1, keepdims=True) + EPS) * g


def _mod_row(mod_ref, grp, col, d):
    return mod_ref[pl.ds(grp, 1), col * d:(col + 1) * d]


def _row_chunks(n_rows, rows, body):
    def step(r, carry):
        body(pl.multiple_of(r * rows, rows))
        return carry
    lax.fori_loop(0, n_rows // rows, step, 0)


def _prenorm_chunk(x, g, shift, scale1):
    return _rms(x, g) * scale1 + shift


def _prenorm_into(h_ref, x_ref, g_ref, mod_ref, grp, shift_col, scale_col, rows=128):
    d = x_ref.shape[-1]
    shift = _mod_row(mod_ref, grp, shift_col, d)
    scale1 = 1.0 + _mod_row(mod_ref, grp, scale_col, d)
    g = g_ref[...]

    def body(r0):
        x = x_ref[pl.ds(r0, rows), :]
        h_ref[pl.ds(r0, rows), :] = _prenorm_chunk(x, g, shift, scale1).astype(h_ref.dtype)

    _row_chunks(x_ref.shape[0], rows, body)


def _mod_kernel(s_ref, w_ref, b_ref, o_ref):
    o_ref[...] = jnp.dot(s_ref[...], w_ref[...], preferred_element_type=F32,
                         precision=lax.Precision.HIGHEST) + b_ref[...]


def _mod_table(s, w_mod, b_mod):
    depth, d, n = w_mod.shape
    tn = 1024
    return pl.pallas_call(
        _mod_kernel,
        out_shape=jax.ShapeDtypeStruct((depth, MOD_ROWS, n), F32),
        grid=(depth, n // tn),
        in_specs=[
            pl.BlockSpec((MOD_ROWS, d), lambda l, j: (0, 0)),
            pl.BlockSpec((None, d, tn), lambda l, j: (l, 0, j)),
            pl.BlockSpec((None, 1, tn), lambda l, j: (l, 0, j)),
        ],
        out_specs=pl.BlockSpec((None, MOD_ROWS, tn), lambda l, j: (l, 0, j)),
        compiler_params=_cparams(("parallel", "parallel")),
        name="mod_table",
    )(s, w_mod, b_mod.reshape(depth, 1, n))


def _in_proj_kernel(x_ref, g_ref, mod_ref, w_ref, o_ref, h_ref, *, tiles_per_batch, n_batch):
    i = pl.program_id(0)

    @pl.when(pl.program_id(1) == 0)
    def _():
        grp = jnp.minimum(i // tiles_per_batch, n_batch)
        _prenorm_into(h_ref, x_ref, g_ref, mod_ref, grp, 0, 1)

    o_ref[...] = jnp.dot(h_ref[...], w_ref[...], preferred_element_type=F32).astype(o_ref.dtype)


def _in_proj(xs, g, mod, w, layer, *, n_batch, seq, tm=1024, tn=2048):
    m, d = xs.shape
    n = w.shape[-1]
    kern = functools.partial(_in_proj_kernel, tiles_per_batch=seq // tm, n_batch=n_batch)
    return pl.pallas_call(
        kern,
        out_shape=jax.ShapeDtypeStruct((m, n), BF16),
        grid=(m // tm, n // tn),
        in_specs=[
            pl.BlockSpec((tm, d), lambda i, j: (i, 0)),
            pl.BlockSpec((None, 1, d), lambda i, j: (layer, 0, 0)),
            pl.BlockSpec((None, MOD_ROWS, N_MOD * d), lambda i, j: (layer, 0, 0)),
            pl.BlockSpec((None, d, tn), lambda i, j: (layer, 0, j)),
        ],
        out_specs=pl.BlockSpec((tm, tn), lambda i, j: (i, j)),
        scratch_shapes=[pltpu.VMEM((tm, d), BF16)],
        compiler_params=_cparams(("parallel", "arbitrary")),
        name="in_proj",
    )(xs, g, mod, w)


def _attn_tables():
    rows = GRID_W
    qi = np.arange(ATT_QROWS)[:, None]
    kr = np.arange(ATT_KROWS)[None, :]
    drs, row_oks = [], []
    for i0, start in ((0, 0), (2 * ATT_QROWS, ATT_QROWS), (rows - ATT_QROWS, rows - ATT_KROWS)):
        i = i0 + qi
        r = start + kr
        r0 = np.clip(i - WIN_H // 2, 0, rows - WIN_H)
        row_oks.append((r >= r0) & (r < r0 + WIN_H))
        drs.append(np.clip(r - i + (WIN_H - 1), 0, 2 * WIN_H - 2))
    qj = np.arange(GRID_W)[:, None]
    kc = np.arange(GRID_W)[None, :]
    ws = np.clip(qj - WIN_W // 2, 0, GRID_W - WIN_W)
    col_ok = (kc >= ws) & (kc < ws + WIN_W)
    dc = np.clip(kc - qj, -(WIN_W - 1), WIN_W - 1) + (WIN_W - 1)
    return np.stack(drs).astype(np.int32), np.stack(row_oks), dc.astype(np.int32), col_ok


def _attn_pair_plan():
    dr, row_ok, _, _ = _attn_tables()
    entries, plan = [], []
    for pat in range(dr.shape[0]):
        per_q = []
        for qi in range(ATT_QROWS):
            per_p = []
            for p in range(ATT_KROWS // 2):
                halves = tuple(int(dr[pat, qi, kr]) if row_ok[pat, qi, kr] else None for kr in (2 * p, 2 * p + 1))
                if halves == (None, None):
                    per_p.append(None)
                    continue
                if halves not in entries:
                    entries.append(halves)
                per_p.append(entries.index(halves))
            per_q.append(per_p)
        plan.append(per_q)
    return entries, plan


_LOG2E = float(np.log2(np.e))


def _attn_bias_slabs(rpb):
    _, _, dc, col_ok = _attn_tables()
    entries, _ = _attn_pair_plan()
    onehot = (dc[:, :, None] == np.arange(2 * WIN_W - 1)[None, None, :]).astype(np.float32)
    t = jnp.einsum("lhdk,jck->lhdjc", rpb.astype(F32), onehot, precision=lax.Precision.HIGHEST)
    t = jnp.where(col_ok, t * _LOG2E, -jnp.inf)
    masked = jnp.full(t.shape[:2] + t.shape[3:], -jnp.inf, F32)
    half = lambda d: masked if d is None else t[:, :, d]
    return jnp.stack([jnp.concatenate([half(a), half(b)], axis=-1) for a, b in entries], axis=2)


def _attn_kernel(q_ref, k_ref, v_ref, kc_ref, vc_ref, slab_ref, o_ref, bias_ref, *,
                 n_qblocks, max_start, with_ctx_queries):
    rb = pl.program_id(2)
    nk = ATT_KROWS * GRID_W
    scale2 = HEAD_DIM ** -0.5 * _LOG2E
    dn = (((1,), (1,)), ((), ()))
    _, plan = _attn_pair_plan()

    def scores(q, k):
        return lax.dot_general(q, k, dn, preferred_element_type=F32) * scale2

    def assemble(pattern):
        masked = jnp.full((GRID_W, 2 * GRID_W), -jnp.inf, F32)
        for hh in range(ATT_HPB):
            for qi in range(ATT_QROWS):
                for p, e in enumerate(plan[pattern][qi]):
                    blk = masked if e is None else slab_ref[hh, e]
                    bias_ref[hh, qi * GRID_W:(qi + 1) * GRID_W, 2 * p * GRID_W:2 * (p + 1) * GRID_W] = blk

    for pattern, first_rb in ((0, 0), (1, 1), (2, n_qblocks - 1)):
        pl.when(rb == first_rb)(functools.partial(assemble, pattern))

    @pl.when(rb < n_qblocks)
    def _():
        start = jnp.clip(rb * ATT_QROWS - WIN_H // 2, 0, max_start)
        tok0 = pl.multiple_of(start * GRID_W, GRID_W)
        heads = [slice(hh * HEAD_DIM, (hh + 1) * HEAD_DIM) for hh in range(ATT_HPB)]

        s_all = [(scores(q_ref[:, hs], k_ref[pl.ds(tok0, nk), hs]) + bias_ref[hh], scores(q_ref[:, hs], kc_ref[:, hs]))
                 for hh, hs in enumerate(heads)]
        for hs, (s_loc, s_ctx) in zip(heads, s_all):
            m = jnp.maximum(jnp.max(s_loc, axis=-1, keepdims=True), jnp.max(s_ctx, axis=-1, keepdims=True))
            p_loc = jnp.exp2(s_loc - m)
            p_ctx = jnp.exp2(s_ctx - m)
            l = jnp.sum(p_loc, axis=-1, keepdims=True) + jnp.sum(p_ctx, axis=-1, keepdims=True)
            o = (jnp.dot(p_loc.astype(BF16), v_ref[pl.ds(tok0, nk), hs], preferred_element_type=F32)
                 + jnp.dot(p_ctx.astype(BF16), vc_ref[:, hs], preferred_element_type=F32))
            o_ref[:, hs] = (o / l).astype(o_ref.dtype)

    if with_ctx_queries:
        @pl.when(rb == n_qblocks)
        def _():
            for hh in range(ATT_HPB):
                hs = slice(hh * HEAD_DIM, (hh + 1) * HEAD_DIM)
                s_ctx = scores(q_ref[:, hs], kc_ref[:, hs])
                p_ctx = jnp.exp2(s_ctx - jnp.max(s_ctx, axis=-1, keepdims=True))
                l = jnp.sum(p_ctx, axis=-1, keepdims=True)
                o = jnp.dot(p_ctx.astype(BF16), vc_ref[:, hs], preferred_element_type=F32)
                o_ref[:, hs] = (o / l).astype(o_ref.dtype)


def _attention(proj, slabs, layer, *, n_batch, seq, ctx_len, with_ctx_queries):
    m = proj.shape[0]
    nq = ATT_QROWS * GRID_W
    assert nq == ctx_len, "context queries reuse the latent query-block shape"
    rows = seq // GRID_W
    n_qblocks = rows // ATT_QROWS
    ctx_blk0 = (n_batch * seq) // ctx_len
    n_hg = N_HEADS // ATT_HPB
    wblk = ATT_HPB * HEAD_DIM

    def qrow(b, rb):
        return jnp.where(rb == n_qblocks, ctx_blk0 + b, b * n_qblocks + rb)

    assert n_qblocks >= 3, "first / interior / last bias patterns need distinct query blocks"
    n_entries = slabs.shape[2]
    kern = functools.partial(_attn_kernel, n_qblocks=n_qblocks, max_start=rows - ATT_KROWS,
                             with_ctx_queries=with_ctx_queries)
    return pl.pallas_call(
        kern,
        out_shape=jax.ShapeDtypeStruct((m if with_ctx_queries else n_batch * seq, N_HEADS * HEAD_DIM), BF16),
        grid=(n_batch, n_hg, n_qblocks + (1 if with_ctx_queries else 0)),
        in_specs=[
            pl.BlockSpec((nq, wblk), lambda b, h, rb: (qrow(b, rb), h)),
            pl.BlockSpec((seq, wblk), lambda b, h, rb: (b, n_hg + h)),
            pl.BlockSpec((seq, wblk), lambda b, h, rb: (b, 2 * n_hg + h)),
            pl.BlockSpec((ctx_len, wblk), lambda b, h, rb: (ctx_blk0 + b, n_hg + h)),
            pl.BlockSpec((ctx_len, wblk), lambda b, h, rb: (ctx_blk0 + b, 2 * n_hg + h)),
            pl.BlockSpec((None, ATT_HPB, n_entries, GRID_W, 2 * GRID_W), lambda b, h, rb: (layer, h, 0, 0, 0)),
        ],
        out_specs=pl.BlockSpec((nq, wblk), lambda b, h, rb: (qrow(b, rb), h)),
        scratch_shapes=[pltpu.VMEM((ATT_HPB, nq, ATT_KROWS * GRID_W), F32)],
        compiler_params=_cparams(("parallel", "parallel", "arbitrary")),
        name="attention",
    )(proj, proj, proj, proj, proj, slabs)


def _conv_kernel(a_ref, g_ref, ap_ref, gp_ref, an_ref, gn_ref, w_ref, b_ref, lg_ref, lb_ref,
                 o_ref, hext_ref, y_ref, *, tiles_per_seq, n_latent_tiles):
    i = pl.program_id(0)
    t = CONV_T
    halo = CONV_HALO
    c = a_ref.shape[-1]
    is_ctx = i >= n_latent_tiles
    pos = i % tiles_per_seq
    at_start = jnp.logical_or(is_ctx, pos == 0)
    at_end = jnp.logical_or(is_ctx, pos == tiles_per_seq - 1)

    def glu(a, g):
        return a.astype(F32) * jax.nn.sigmoid(g.astype(F32))

    lanes = 128
    rchunk = 64
    off = halo - CONV_K // 2

    hext_ref[0:halo, :] = jnp.where(at_start, 0.0, glu(ap_ref[...], gp_ref[...]))
    hext_ref[halo + t:halo + t + halo, :] = jnp.where(at_end, 0.0, glu(an_ref[...], gn_ref[...]))

    def glu_body(r0):
        dst = pl.multiple_of(r0 + halo, halo)
        hext_ref[pl.ds(dst, rchunk), :] = glu(a_ref[pl.ds(r0, rchunk), :], g_ref[pl.ds(r0, rchunk), :])

    _row_chunks(t, rchunk, glu_body)

    def chunk_body(ci, carry):
        c0 = pl.multiple_of(ci * lanes, lanes)
        wts = w_ref[:, pl.ds(c0, lanes)]
        bias = b_ref[:, pl.ds(c0, lanes)]
        for r0 in range(0, t, rchunk):
            acc = bias
            for res in range(8):
                n_rows = rchunk + (8 if res else 0)
                part = None
                for base in range(0, CONV_K + off, 8):
                    k = base + res - off
                    if 0 <= k < CONV_K:
                        term = wts[k:k + 1, :] * hext_ref[r0 + base:r0 + base + n_rows, pl.ds(c0, lanes)]
                        part = term if part is None else part + term
                if part is not None:
                    acc = acc + part[res:res + rchunk, :]
            y_ref[r0:r0 + rchunk, pl.ds(c0, lanes)] = acc
        return carry

    lax.fori_loop(0, c // lanes, chunk_body, 0)

    ln_g = lg_ref[...]
    ln_b = lb_ref[...]

    def ln_body(r0):
        y = y_ref[pl.ds(r0, rchunk), :]
        mu = jnp.mean(y, axis=-1, keepdims=True)
        yc = y - mu
        var = jnp.mean(yc * yc, axis=-1, keepdims=True)
        z = yc * lax.rsqrt(var + EPS) * ln_g + ln_b
        o_ref[pl.ds(r0, rchunk), :] = _silu(z).astype(o_ref.dtype)

    _row_chunks(t, rchunk, ln_body)


def _conv_module(proj, w_dw, b_dw, ln_g, ln_b, layer, *, n_batch, seq, ctx_len, a_col, g_col, m_rows):
    m = m_rows
    c = w_dw.shape[-1]
    t = CONV_T
    assert ctx_len == t and seq % t == 0
    hb = t // CONV_HALO
    last_hblk = proj.shape[0] // CONV_HALO - 1
    n_tiles = m // t
    kern = functools.partial(_conv_kernel, tiles_per_seq=seq // t, n_latent_tiles=(n_batch * seq) // t)

    def prev(i):
        return jnp.maximum(i * hb - 1, 0)

    def nxt(i):
        return jnp.minimum(i * hb + hb, last_hblk)

    vec = lambda: pl.BlockSpec((None, 1, c), lambda i: (layer, 0, 0))
    return pl.pallas_call(
        kern,
        out_shape=jax.ShapeDtypeStruct((m, c), BF16),
        grid=(n_tiles,),
        in_specs=[
            pl.BlockSpec((t, c), lambda i: (i, a_col)),
            pl.BlockSpec((t, c), lambda i: (i, g_col)),
            pl.BlockSpec((CONV_HALO, c), lambda i: (prev(i), a_col)),
            pl.BlockSpec((CONV_HALO, c), lambda i: (prev(i), g_col)),
            pl.BlockSpec((CONV_HALO, c), lambda i: (nxt(i), a_col)),
            pl.BlockSpec((CONV_HALO, c), lambda i: (nxt(i), g_col)),
            pl.BlockSpec((None, CONV_K, c), lambda i: (layer, 0, 0)),
            vec(), vec(), vec(),
        ],
        out_specs=pl.BlockSpec((t, c), lambda i: (i, 0)),
        scratch_shapes=[pltpu.VMEM((t + 2 * CONV_HALO, c), F32), pltpu.VMEM((t, c), F32)],
        compiler_params=_cparams(("parallel",)),
        name="conv_module",
    )(proj, proj, proj, proj, proj, proj, w_dw, b_dw, ln_g, ln_b)


def _merge_kernel(oa_ref, hc_ref, wa_ref, wc_ref, ga_ref, gc_ref, o_ref):
    ya = jnp.dot(oa_ref[...], wa_ref[...], preferred_element_type=F32)
    yc = jnp.dot(hc_ref[...], wc_ref[...], preferred_element_type=F32)
    y = jax.nn.sigmoid(ga_ref[...].astype(F32)) * ya + jax.nn.sigmoid(gc_ref[...].astype(F32)) * yc
    o_ref[...] = y.astype(o_ref.dtype)


def _merge(o_attn, h_conv, wa, wc, proj, layer, *, ga_col0, gc_col0, tm=512, tn=1024):
    m, k = o_attn.shape
    n = wa.shape[-1]
    return pl.pallas_call(
        _merge_kernel,
        out_shape=jax.ShapeDtypeStruct((m, n), BF16),
        grid=(n // tn, m // tm),
        in_specs=[
            pl.BlockSpec((tm, k), lambda j, i: (i, 0)),
            pl.BlockSpec((tm, k), lambda j, i: (i, 0)),
            pl.BlockSpec((None, k, tn), lambda j, i: (layer, 0, j)),
            pl.BlockSpec((None, k, tn), lambda j, i: (layer, 0, j)),
            pl.BlockSpec((tm, tn), lambda j, i: (i, ga_col0 // tn + j)),
            pl.BlockSpec((tm, tn), lambda j, i: (i, gc_col0 // tn + j)),
        ],
        out_specs=pl.BlockSpec((tm, tn), lambda j, i: (i, j)),
        compiler_params=_cparams(("parallel", "parallel")),
        name="merge",
    )(o_attn, h_conv, wa, wc, proj, proj)


def _post_residual(z_chunk, x_ref, g_ref, mod_ref, grp, gate_col, o_ref, rows=128):
    d = x_ref.shape[-1]
    gate = _mod_row(mod_ref, grp, gate_col, d)
    g = g_ref[...]

    def body(r0):
        o_ref[pl.ds(r0, rows), :] = x_ref[pl.ds(r0, rows), :] + gate * _rms(z_chunk(r0), g)

    _row_chunks(x_ref.shape[0], rows, body)


def _matmul_res_kernel(a_ref, w_ref, x_ref, g_ref, mod_ref, o_ref, acc0_ref, acc1_ref, *,
                       tiles_per_batch, n_batch, gate_col, nk, n_tiles):
    i = pl.program_id(0)
    kk = pl.program_id(1)
    tm, d = x_ref.shape
    rows = tm // nk
    gate = _mod_row(mod_ref, jnp.minimum(jnp.maximum(i - 1, 0) // tiles_per_batch, n_batch), gate_col, d)

    @pl.when(jnp.logical_and(i == 0, kk == 0))
    def _():
        acc1_ref[...] = jnp.zeros_like(acc1_ref)

    for parity, (acc_w, acc_r) in enumerate(((acc0_ref, acc1_ref), (acc1_ref, acc0_ref))):
        for k_step in range(nk):
            @pl.when(jnp.logical_and(i % 2 == parity, kk == k_step))
            def _(acc_w=acc_w, acc_r=acc_r, k_step=k_step):
                part = jnp.dot(a_ref[...], w_ref[...], preferred_element_type=F32)
                if k_step == 0:
                    acc_w[...] = part
                else:
                    acc_w[...] += part
                sub = min(rows, 128)
                for r0 in range(k_step * rows, (k_step + 1) * rows, sub):
                    z = acc_r[r0:r0 + sub, :]
                    o_ref[r0:r0 + sub, :] = x_ref[r0:r0 + sub, :] + gate * _rms(z, g_ref[...])


def _matmul_residual(a, w, xs, g, mod, layer, *, gate_col, n_batch, seq, m_rows, w_index, tm=512, tk=None):
    k = a.shape[1]
    d = xs.shape[1]
    tk = k if tk is None else tk
    nk = k // tk
    n_tiles = m_rows // tm
    kern = functools.partial(_matmul_res_kernel, tiles_per_batch=seq // tm, n_batch=n_batch,
                             gate_col=gate_col, nk=nk, n_tiles=n_tiles)
    cur = lambda i: jnp.minimum(i, n_tiles - 1)
    prev = lambda i: jnp.maximum(i - 1, 0)
    return pl.pallas_call(
        kern,
        out_shape=jax.ShapeDtypeStruct(xs.shape, F32),
        grid=(n_tiles + 1, nk),
        in_specs=[
            pl.BlockSpec((tm, tk), lambda i, kk: (cur(i), kk)),
            pl.BlockSpec((None, tk, d), lambda i, kk: (w_index, kk, 0)),
            pl.BlockSpec((tm, d), lambda i, kk: (prev(i), 0)),
            pl.BlockSpec((None, 1, d), lambda i, kk: (layer, 0, 0)),
            pl.BlockSpec((None, MOD_ROWS, N_MOD * d), lambda i, kk: (layer, 0, 0)),
        ],
        out_specs=pl.BlockSpec((tm, d), lambda i, kk: (prev(i), 0)),
        scratch_shapes=[pltpu.VMEM((tm, d), F32), pltpu.VMEM((tm, d), F32)],
        input_output_aliases={2: 0},
        compiler_params=_cparams(("arbitrary", "arbitrary")),
        name="matmul_residual",
    )(a, w, xs, g, mod)


def _ffn_in_kernel(x_ref, g_ref, mod_ref, wg_ref, wu_ref, o_ref, h_ref, *, tiles_per_batch, n_batch):
    i = pl.program_id(0)

    @pl.when(pl.program_id(1) == 0)
    def _():
        grp = jnp.minimum(i // tiles_per_batch, n_batch)
        _prenorm_into(h_ref, x_ref, g_ref, mod_ref, grp, 3, 4)

    h = h_ref[...]
    a = jnp.dot(h, wg_ref[...], preferred_element_type=F32)
    u = jnp.dot(h, wu_ref[...], preferred_element_type=F32)
    o_ref[...] = (_silu(a) * u).astype(o_ref.dtype)


def _ffn_in(xs, g, mod, wg, wu, layer, j_dense, *, n_batch, seq, m_rows, tm=1024, tn=512):
    d = xs.shape[1]
    f = wg.shape[-1]
    kern = functools.partial(_ffn_in_kernel, tiles_per_batch=seq // tm, n_batch=n_batch)
    return pl.pallas_call(
        kern,
        out_shape=jax.ShapeDtypeStruct((m_rows, f), BF16),
        grid=(m_rows // tm, f // tn),
        in_specs=[
            pl.BlockSpec((tm, d), lambda i, j: (i, 0)),
            pl.BlockSpec((None, 1, d), lambda i, j: (layer, 0, 0)),
            pl.BlockSpec((None, MOD_ROWS, N_MOD * d), lambda i, j: (layer, 0, 0)),
            pl.BlockSpec((None, d, tn), lambda i, j: (j_dense, 0, j)),
            pl.BlockSpec((None, d, tn), lambda i, j: (j_dense, 0, j)),
        ],
        out_specs=pl.BlockSpec((tm, tn), lambda i, j: (i, j)),
        scratch_shapes=[pltpu.VMEM((tm, d), BF16)],
        compiler_params=_cparams(("parallel", "arbitrary")),
        name="ffn_in",
    )(xs, g, mod, wg, wu)


def _split_bf16(a):
    hi = a.astype(BF16)
    return hi, (a - hi.astype(F32)).astype(BF16)


def _router_kernel(x_ref, g_ref, mod_ref, wh_ref, wl_ref, h_ref, top_ref, *, tiles_per_batch, n_batch):
    i = pl.program_id(0)
    grp = jnp.minimum(i // tiles_per_batch, n_batch)
    d = x_ref.shape[-1]
    shift = _mod_row(mod_ref, grp, 3, d)
    scale1 = 1.0 + _mod_row(mod_ref, grp, 4, d)
    g = g_ref[...]
    rows = 256

    def body(r0):
        h = _prenorm_chunk(x_ref[pl.ds(r0, rows), :], g, shift, scale1)
        h_ref[pl.ds(r0, rows), :] = h.astype(h_ref.dtype)
        h_hi, h_lo = _split_bf16(h)
        dot = functools.partial(jnp.dot, preferred_element_type=F32)
        logits = dot(h_hi, wh_ref[...]) + (dot(h_hi, wl_ref[...]) + dot(h_lo, wh_ref[...]))
        lane =lax.broadcasted_iota(jnp.int32, logits.shape, 1)
        big = jnp.int32(logits.shape[-1])
        logits = jnp.where(lane < N_EXPERTS, logits, -jnp.inf)
        m1 = jnp.max(logits, axis=-1, keepdims=True)
        i1 = jnp.min(jnp.where(logits == m1, lane, big), axis=-1, keepdims=True)
        rest = jnp.where(lane == i1, -jnp.inf, logits)
        m2 = jnp.max(rest, axis=-1, keepdims=True)
        i2 = jnp.min(jnp.where(rest == m2, lane, big), axis=-1, keepdims=True)
        e2 = jnp.exp(m2 - m1)
        w1 = 1.0 / (1.0 + e2)
        w2 = e2 / (1.0 + e2)
        packed = jnp.where(lane == 0, i1.astype(F32), jnp.where(lane == 1, i2.astype(F32),
                           jnp.where(lane == 2, w1, jnp.where(lane == 3, w2, 0.0))))
        top_ref[:, pl.ds(r0, rows)] = jnp.transpose(packed)[:ROUTE_ROWS, :]

    _row_chunks(x_ref.shape[0], rows, body)


def _router(xs, g, mod, w_hi, w_lo, layer, j_moe, *, n_batch, seq, m_rows, tm=512):
    d = xs.shape[1]
    lanes = w_hi.shape[-1]
    kern = functools.partial(_router_kernel, tiles_per_batch=seq // tm, n_batch=n_batch)
    return pl.pallas_call(
        kern,
        out_shape=(jax.ShapeDtypeStruct((m_rows, d), BF16),
                   jax.ShapeDtypeStruct((ROUTE_ROWS, m_rows), F32)),
        grid=(m_rows // tm,),
        in_specs=[
            pl.BlockSpec((tm, d), lambda i: (i, 0)),
            pl.BlockSpec((None, 1, d), lambda i: (layer, 0, 0)),
            pl.BlockSpec((None, MOD_ROWS, N_MOD * d), lambda i: (layer, 0, 0)),
            pl.BlockSpec((None, d, lanes), lambda i: (j_moe, 0, 0)),
            pl.BlockSpec((None, d, lanes), lambda i: (j_moe, 0, 0)),
        ],
        out_specs=(pl.BlockSpec((tm, d), lambda i: (i, 0)),
                   pl.BlockSpec((ROUTE_ROWS, tm), lambda i: (0, i))),
        compiler_params=_cparams(("parallel",)),
        name="router",
    )(xs, g, mod, w_hi, w_lo)


def _expert_block_is_new(te_ref, t):
    return jnp.logical_or(t == 0, te_ref[t] != te_ref[jnp.maximum(t - 1, 0)])


def _tile_cases(rows_ref, t, tm, full, half, empty):
    rows = rows_ref[t]
    pl.when(rows > tm // 2)(full)
    pl.when(jnp.logical_and(rows > 0, rows <= tm // 2))(half)
    pl.when(rows == 0)(empty)


def _moe_up_kernel(te_ref, rows_ref, x_ref, wg_ref, wu_ref, o_ref, wgb_ref, wub_ref):
    t = pl.program_id(1)
    tm = x_ref.shape[0]

    @pl.when(_expert_block_is_new(te_ref, t))
    def _():
        wgb_ref[...] = wg_ref[...].astype(BF16)
        wub_ref[...] = wu_ref[...].astype(BF16)

    def compute(n_rows):
        x = x_ref[:n_rows, :]
        a = jnp.dot(x, wgb_ref[...], preferred_element_type=F32)
        u = jnp.dot(x, wub_ref[...], preferred_element_type=F32)
        o_ref[:n_rows, :] = (_silu(a) * u).astype(o_ref.dtype)

    def half():
        compute(tm // 2)
        o_ref[tm // 2:, :] = jnp.zeros((tm - tm // 2, o_ref.shape[1]), o_ref.dtype)

    def empty():
        o_ref[...] = jnp.zeros_like(o_ref)

    _tile_cases(rows_ref, t, tm, functools.partial(compute, tm), half, empty)


def _moe_up(tile_expert, tile_rows, x_sorted, wg, wu, j_moe, *, tn=512):
    r, d = x_sorted.shape
    f = wg.shape[-1]
    tm = MOE_TM
    grid_spec = pltpu.PrefetchScalarGridSpec(
        num_scalar_prefetch=2,
        grid=(f // tn, r // tm),
        in_specs=[
            pl.BlockSpec((tm, d), lambda j, t, te, nu: (t, 0)),
            pl.BlockSpec((None, None, d, tn), lambda j, t, te, nu: (j_moe, te[t], 0, j)),
            pl.BlockSpec((None, None, d, tn), lambda j, t, te, nu: (j_moe, te[t], 0, j)),
        ],
        out_specs=pl.BlockSpec((None, tm, tn), lambda j, t, te, nu: (j, t, 0)),
        scratch_shapes=[pltpu.VMEM((d, tn), BF16), pltpu.VMEM((d, tn), BF16)],
    )
    return pl.pallas_call(
        _moe_up_kernel,
        out_shape=jax.ShapeDtypeStruct((f // tn, r, tn), BF16),
        grid_spec=grid_spec,
        compiler_params=_cparams(("arbitrary", "arbitrary")),
        name="moe_up",
    )(tile_expert, tile_rows, x_sorted, wg, wu)


def _moe_down_kernel(te_ref, rows_ref, a_ref, wd_ref, rw_ref, o_ref, wdb_ref, *, tiles_per_up_tile):
    t = pl.program_id(1)
    n_chunks, tm, tk = a_ref.shape
    up_tile = t // tiles_per_up_tile
    rows = jnp.clip(rows_ref[up_tile] - (t % tiles_per_up_tile) * tm, 0, tm)

    @pl.when(_expert_block_is_new(te_ref, up_tile) & (t % tiles_per_up_tile == 0))
    def _():
        wdb_ref[...] = wd_ref[...].astype(BF16)

    @pl.when(rows > 0)
    def _():
        y = jnp.dot(a_ref[0], wdb_ref[0:tk, :], preferred_element_type=F32)
        for c in range(1, n_chunks):
            y = y + jnp.dot(a_ref[c], wdb_ref[c * tk:(c + 1) * tk, :], preferred_element_type=F32)
        o_ref[...] = (y * rw_ref[...]).astype(o_ref.dtype)

    @pl.when(rows == 0)
    def _():
        o_ref[...] = jnp.zeros_like(o_ref)


def _moe_down(tile_expert, tile_rows, act, wd, row_w, j_moe, *, tm=512, tn=1024):
    n_chunks, r, tk = act.shape
    f = n_chunks * tk
    d = wd.shape[-1]
    per_up = MOE_TM // tm
    kern = functools.partial(_moe_down_kernel, tiles_per_up_tile=per_up)
    grid_spec = pltpu.PrefetchScalarGridSpec(
        num_scalar_prefetch=2,
        grid=(d // tn, r // tm),
        in_specs=[
            pl.BlockSpec((n_chunks, tm, tk), lambda j, t, te, nu: (0, t, 0)),
            pl.BlockSpec((None, None, f, tn), lambda j, t, te, nu: (j_moe, te[t // per_up], 0, j)),
            pl.BlockSpec((tm, 1), lambda j, t, te, nu: (t, 0)),
        ],
        out_specs=pl.BlockSpec((tm, tn), lambda j, t, te, nu: (t, j)),
        scratch_shapes=[pltpu.VMEM((f, tn), BF16)],
    )
    return pl.pallas_call(
        kern,
        out_shape=jax.ShapeDtypeStruct((r, d), BF16),
        grid_spec=grid_spec,
        compiler_params=_cparams(("arbitrary", "arbitrary")),
        name="moe_down",
    )(tile_expert, tile_rows, act, wd, row_w)


def _norm_res_kernel(y0_ref, y1_ref, x_ref, g_ref, mod_ref, o_ref, *, tiles_per_batch, n_batch, gate_col):
    i = pl.program_id(0)
    grp = jnp.minimum(i // tiles_per_batch, n_batch)
    rows = 128
    _post_residual(lambda r0: y0_ref[pl.ds(r0, rows), :].astype(F32) + y1_ref[pl.ds(r0, rows), :].astype(F32),
                   x_ref, g_ref, mod_ref, grp, gate_col, o_ref, rows)


def _norm_residual(y0, y1, xs, g, mod, layer, *, gate_col, n_batch, seq, m_rows, in_place, tm=512):
    d = xs.shape[1]
    kern = functools.partial(_norm_res_kernel, tiles_per_batch=seq // tm, n_batch=n_batch, gate_col=gate_col)
    row = lambda: pl.BlockSpec((tm, d), lambda i: (i, 0))
    return pl.pallas_call(
        kern,
        out_shape=jax.ShapeDtypeStruct(xs.shape if in_place else (m_rows, d), F32),
        grid=(m_rows // tm,),
        in_specs=[row(), row(), row(),
                  pl.BlockSpec((None, 1, d), lambda i: (layer, 0, 0)),
                  pl.BlockSpec((None, MOD_ROWS, N_MOD * d), lambda i: (layer, 0, 0))],
        out_specs=row(),
        input_output_aliases={2: 0} if in_place else {},
        compiler_params=_cparams(("parallel",)),
        name="norm_residual",
    )(y0, y1, xs, g, mod)


RANK_CHUNK = 512


def _rank_kernel(e_ref, tri_ref, rank_ref, cnt_ref, carry_ref):
    @pl.when(pl.program_id(0) == 0)
    def _():
        carry_ref[...] = jnp.zeros_like(carry_ref)

    e = e_ref[...]
    expert = lax.broadcasted_iota(jnp.int32, (N_EXPERTS, e.shape[1]), 0)
    onehot = (expert == e).astype(F32)
    local = jnp.dot(onehot.astype(BF16), tri_ref[...], preferred_element_type=F32)
    carry = carry_ref[...]
    csum = local + carry[:, :1]
    rank_ref[...] = (jnp.sum(csum * onehot, axis=0, keepdims=True) - 1.0).astype(jnp.int32)
    carry = carry + jnp.sum(onehot, axis=1, keepdims=True)
    carry_ref[...] = carry
    cnt_ref[...] = carry.astype(jnp.int32)


def _pair_ranks(flat_e):
    n2 = flat_e.shape[0]
    c = RANK_CHUNK
    tri = jnp.asarray(np.triu(np.ones((c, c), np.float32)), BF16)
    rank, cnt = pl.pallas_call(
        _rank_kernel,
        out_shape=(jax.ShapeDtypeStruct((1, n2), jnp.int32), jax.ShapeDtypeStruct((N_EXPERTS, 128), jnp.int32)),
        grid=(n2 // c,),
        in_specs=[pl.BlockSpec((1, c), lambda s: (0, s)), pl.BlockSpec((c, c), lambda s: (0, 0))],
        out_specs=(pl.BlockSpec((1, c), lambda s: (0, s)), pl.BlockSpec((N_EXPERTS, 128), lambda s: (0, 0))),
        scratch_shapes=[pltpu.VMEM((N_EXPERTS, 128), F32)],
        compiler_params=_cparams(("arbitrary",)),
        name="pair_ranks",
    )(flat_e.reshape(1, n2), tri)
    return rank.reshape(n2), cnt[:, 0]


def _route_tables(top, tm):
    n = top.shape[1]
    flat_e = top[0:2].astype(jnp.int32).reshape(-1)
    flat_w = top[2:4].reshape(-1)
    rank, counts = _pair_ranks(flat_e)
    padded = ((counts + tm - 1) // tm) * tm
    ends = jnp.cumsum(padded)
    starts = ends - padded
    pos = starts[flat_e] + rank
    r_pad = 2 * n + N_EXPERTS * tm
    n_tiles = r_pad // tm
    row_pair = jnp.full((r_pad,), 2 * n, jnp.int32).at[pos].set(jnp.arange(2 * n, dtype=jnp.int32))
    is_real = row_pair < 2 * n
    spread = jnp.arange(r_pad, dtype=jnp.int32) % (2 * n)
    row_pair = jnp.where(is_real, row_pair, spread)
    row_token = row_pair % n
    row_w = jnp.where(is_real, flat_w[row_pair], 0.0)
    n_used = (ends[-1] // tm).astype(jnp.int32)
    tile_start = jnp.arange(n_tiles, dtype=jnp.int32) * tm
    tile_expert = jnp.sum((tile_start[:, None] >= ends[None, :]).astype(jnp.int32), axis=1)
    tile_expert = jnp.minimum(tile_expert, N_EXPERTS - 1)
    tile_rows = jnp.clip(counts[tile_expert] - (tile_start - starts[tile_expert]), 0, tm)
    tile_rows = jnp.where(jnp.arange(n_tiles) < n_used, tile_rows, 0)
    last = tile_expert[jnp.maximum(n_used - 1, 0)]
    tile_expert = jnp.where(jnp.arange(n_tiles) < n_used, tile_expert, last)
    return (pos.reshape(2, n), row_token, row_w, tile_expert.astype(jnp.int32), tile_rows.astype(jnp.int32))


def kernel(x, c, ctx, c_ctx, w_mod, b_mod, g_mix_pre, g_mix_post, g_ffn_pre, g_ffn_post, w_in, rpb,
           w_attn_out, conv_dw, conv_db, conv_ln_g, conv_ln_b, w_conv_out, w_out, w_ff_gate, w_ff_up,
           w_ff_down, w_router, w_exp_gate, w_exp_up, w_exp_down):
    n_batch, seq, d = x.shape
    ctx_len = ctx.shape[1]
    depth = w_mod.shape[0]
    n_lat = n_batch * seq
    m_all = n_lat + n_batch * ctx_len
    assert seq == GRID_W * GRID_W and d == N_HEADS * HEAD_DIM and n_batch + 1 <= MOD_ROWS
    geo = dict(n_batch=n_batch, seq=seq)

    w_in_b = w_in.astype(BF16)
    w_ao_b = w_attn_out.astype(BF16)
    w_co_b = w_conv_out.astype(BF16)
    w_o_b = w_out.astype(BF16)
    w_fg_b, w_fu_b, w_fd_b = w_ff_gate.astype(BF16), w_ff_up.astype(BF16), w_ff_down.astype(BF16)
    w_router_hi, w_router_lo = _split_bf16(jnp.pad(w_router, ((0, 0), (0, 0), (0, 128 - N_EXPERTS))))

    vec = lambda a: a.reshape(depth, 1, -1)
    g_mix_pre, g_mix_post, g_ffn_pre, g_ffn_post = map(vec, (g_mix_pre, g_mix_post, g_ffn_pre, g_ffn_post))
    conv_db, conv_ln_g, conv_ln_b = map(vec, (conv_db, conv_ln_g, conv_ln_b))

    s = jnp.zeros((MOD_ROWS, d), F32).at[:n_batch].set(_silu(c)).at[n_batch].set(_silu(c_ctx))
    mod = _mod_table(s, w_mod, b_mod)
    slabs = _attn_bias_slabs(rpb)

    xs = jnp.concatenate([x.reshape(n_lat, d), ctx.reshape(n_batch * ctx_len, d)], axis=0)
    attn_w = N_HEADS * HEAD_DIM
    conv_w = conv_dw.shape[-1]
    u_col0 = 3 * attn_w
    ga_col0 = u_col0 + 2 * conv_w
    gc_col0 = ga_col0 + d

    for layer in range(depth):
        last = layer == depth - 1
        m_rows = n_lat if last else m_all
        j = layer // 2

        proj = _in_proj(xs, g_mix_pre, mod, w_in_b, layer, **geo)
        o_attn = _attention(proj, slabs, layer, ctx_len=ctx_len, with_ctx_queries=not last, **geo)
        h_conv = _conv_module(proj, conv_dw, conv_db, conv_ln_g, conv_ln_b, layer, ctx_len=ctx_len,
                              a_col=u_col0 // conv_w, g_col=u_col0 // conv_w + 1, m_rows=m_rows, **geo)
        y = _merge(o_attn, h_conv, w_ao_b, w_co_b, proj, layer, ga_col0=ga_col0, gc_col0=gc_col0)
        xs = _matmul_residual(y, w_o_b, xs, g_mix_post, mod, layer, gate_col=2, m_rows=m_rows,
                              w_index=layer, **geo)

        if layer % 2 == 0:
            act = _ffn_in(xs, g_ffn_pre, mod, w_fg_b, w_fu_b, layer, j, m_rows=m_rows, **geo)
            xs = _matmul_residual(act, w_fd_b, xs, g_ffn_post, mod, layer, gate_col=5, m_rows=m_rows,
                                  w_index=j, tk=2816, **geo)
        else:
            h, top = _router(xs, g_ffn_pre, mod, w_router_hi, w_router_lo, layer, j, m_rows=m_rows, **geo)
            pos, row_token, row_w, tile_expert, tile_rows = _route_tables(top, MOE_TM)
            x_sorted = jnp.take(h, row_token, axis=0, mode="clip")
            act = _moe_up(tile_expert, tile_rows, x_sorted, w_exp_gate, w_exp_up, j)
            y_sorted = _moe_down(tile_expert, tile_rows, act, w_exp_down, row_w[:, None], j)
            y0 = jnp.take(y_sorted, pos[0], axis=0, mode="clip")
            y1 = jnp.take(y_sorted, pos[1], axis=0, mode="clip")
            xs = _norm_residual(y0, y1, xs, g_ffn_post, mod, layer, gate_col=5, m_rows=m_rows,
                                in_place=not last, **geo)

    return xs[:n_lat].reshape(n_batch, seq, d)
```
